```python
import math
import jax, jax.numpy as jnp
from jax import lax
import numpy as np

D_MODEL = 1024
BATCH = 8
SEQ = 2048
DEPTH = 1
DEC_BATCH = 32
DEC_SEQ = 1
PAST_LEN = 16384
PAGE_SIZE = 128

MLA_HEADS = 8
MLA_NOPE = 64
MLA_ROPE = 32
MLA_V = 64
KV_RANK = 128
ROPE_THETA = 10000.0
FOX_HEADS = 8
FOX_DIM = 64
D_MIX = MLA_HEADS * MLA_V + FOX_HEADS * FOX_DIM
Q_BLOCK = 128
SPLITS = (MLA_HEADS * MLA_NOPE, MLA_HEADS * MLA_ROPE, KV_RANK, MLA_ROPE,
          FOX_HEADS * FOX_DIM, FOX_HEADS * FOX_DIM, FOX_HEADS * FOX_DIM, FOX_HEADS)
D_IN_PROJ = MLA_HEADS * (MLA_NOPE + MLA_ROPE) + KV_RANK + MLA_ROPE + 3 * FOX_HEADS * FOX_DIM + FOX_HEADS
PEER_HEADS = 8
PEER_KEYS = 128
PEER_EXPERTS = PEER_KEYS * PEER_KEYS
PEER_DKEY = 256
PEER_TOPK = 16
PEER_BLOCK = 128
NORM_EPS = 1e-6
NEG_INF = -1e30

kernel_name = 'hybrid_mla_fox_peer_step'


def rmsnorm(x, g):
    x32 = x.astype(jnp.float32)
    y = x32 * lax.rsqrt(jnp.mean(x32 * x32, axis=-1, keepdims=True) + NORM_EPS)
    return (y * g.astype(jnp.float32)).astype(x.dtype)


def rope(x, pos):
    half = MLA_ROPE // 2
    inv = jnp.power(ROPE_THETA, -jnp.arange(half, dtype=jnp.float32) / half)
    ang = pos.astype(jnp.float32)[:, None] * inv
    if x.ndim == 4:
        ang = ang[:, None, :]
    cos, sin = jnp.cos(ang), jnp.sin(ang)
    x32 = x.astype(jnp.float32)
    x1, x2 = x32[..., :half], x32[..., half:]
    return jnp.concatenate([x1 * cos - x2 * sin, x1 * sin + x2 * cos], axis=-1).astype(x.dtype)


def causal_block_attention(q_parts, k_parts, v, q_pos, k_pos, scale, c_q=None, c_k=None):
    B, T, H, _ = q_parts[0].shape
    qb = min(Q_BLOCK, T)
    nb = T // qb

    def blocks(a):
        return jnp.swapaxes(a.reshape((B, nb, qb) + a.shape[2:]), 0, 1)

    xs = (tuple(blocks(q) for q in q_parts), q_pos.reshape(nb, qb))
    if c_q is not None:
        xs = xs + (blocks(c_q),)
        ck = jnp.transpose(c_k, (0, 2, 1))[:, :, None, :]

    def one_block(args):
        qs, pos = args[0], args[1]
        s = None
        for qi, ki in zip(qs, k_parts):
            spec = 'bqhd,bsd->bhqs' if ki.ndim == 3 else 'bqhd,bshd->bhqs'
            term = jnp.einsum(spec, qi, ki, preferred_element_type=jnp.float32)
            s = term if s is None else s + term
        s = s * scale
        if c_q is not None:
            s = s + jnp.transpose(args[2].astype(jnp.float32), (0, 2, 1))[..., None] - ck
        mask = k_pos[None, :] <= pos[:, None]
        s = jnp.where(mask[None, None], s, NEG_INF)
        p = jax.nn.softmax(s, axis=-1)
        return jnp.einsum('bhqs,bshd->bqhd', p.astype(v.dtype), v)

    out = lax.map(one_block, xs)
    return jnp.swapaxes(out, 0, 1).reshape(B, T, H, v.shape[-1])


def token_mixers(xn, past, w_in, b_forget, g_q_nope, g_q_pe, g_ckv, g_k_pe, w_ukv, g_k_nope,
                 g_fox_q, g_fox_k, g_out_mla, g_out_fox, w_o):
    B, T, _ = xn.shape
    n_past = 0 if past is None else past[0].shape[1]
    pos = n_past + jnp.arange(T, dtype=jnp.int32)
    proj = jnp.einsum('btd,de->bte', xn, w_in)
    split_points = np.cumsum(SPLITS)[:-1].tolist()
    q_nope, q_pe, ckv, kpe, fq, fk, fv, f_logit = jnp.split(proj, split_points, axis=-1)
    q_nope = rmsnorm(q_nope.reshape(B, T, MLA_HEADS, MLA_NOPE), g_q_nope)
    q_pe = rope(rmsnorm(q_pe.reshape(B, T, MLA_HEADS, MLA_ROPE), g_q_pe), pos)
    ckv = rmsnorm(ckv, g_ckv)
    kpe = rope(rmsnorm(kpe, g_k_pe), pos)
    fq = rmsnorm(fq.reshape(B, T, FOX_HEADS, FOX_DIM), g_fox_q)
    fk = rmsnorm(fk.reshape(B, T, FOX_HEADS, FOX_DIM), g_fox_k)
    fv = fv.reshape(B, T, FOX_HEADS, FOX_DIM)
    logf = jax.nn.log_sigmoid((f_logit + b_forget).astype(jnp.float32))
    if past is None:
        ckv_all, kpe_all, fk_all, fv_all, logf_all = ckv, kpe, fk, fv, logf
    else:
        p_ckv, p_kpe, p_fk, p_fv, p_logf = past
        ckv_all = jnp.concatenate([p_ckv, ckv], axis=1)
        kpe_all = jnp.concatenate([p_kpe, kpe], axis=1)
        fk_all = jnp.concatenate([p_fk, fk], axis=1)
        fv_all = jnp.concatenate([p_fv, fv], axis=1)
        logf_all = jnp.concatenate([p_logf.astype(jnp.float32), logf], axis=1)
    S = n_past + T
    k_pos = jnp.arange(S, dtype=jnp.int32)
    kv = jnp.einsum('bsr,re->bse', ckv_all, w_ukv).reshape(B, S, MLA_HEADS, MLA_NOPE + MLA_V)
    k_nope = rmsnorm(kv[..., :MLA_NOPE], g_k_nope)
    v_mla = kv[..., MLA_NOPE:]
    o_mla = causal_block_attention((q_nope, q_pe), (k_nope, kpe_all), v_mla, pos, k_pos,
                                   (MLA_NOPE + MLA_ROPE) ** -0.5)
    c = jnp.cumsum(logf_all.astype(jnp.float32), axis=1)
    o_fox = causal_block_attention((fq,), (fk_all,), fv_all, pos, k_pos, FOX_DIM ** -0.5,
                                   c_q=c[:, n_past:], c_k=c)
    mixed = jnp.concatenate([rmsnorm(o_mla.reshape(B, T, MLA_HEADS * MLA_V), g_out_mla),
                             rmsnorm(o_fox.reshape(B, T, FOX_HEADS * FOX_DIM), g_out_fox)], axis=-1)
    out = jnp.einsum('bte,ed->btd', mixed, w_o)
    return out, (ckv, kpe, fk, fv, logf)


def peer_ffn(xn, w_peer_q, peer_sub_keys, peer_u, peer_v):
    B, T, D = xn.shape
    N = B * T
    n_blocks = -(-N // PEER_BLOCK)
    pad = n_blocks * PEER_BLOCK - N
    xf = jnp.pad(xn.reshape(N, D), ((0, pad), (0, 0))).reshape(n_blocks, PEER_BLOCK, D)

    def one_block(xb):
        q = jnp.einsum('td,de->te', xb, w_peer_q).reshape(PEER_BLOCK, PEER_HEADS, 2, PEER_DKEY // 2)
        s = jnp.einsum('thpc,hpkc->thpk', q, peer_sub_keys, preferred_element_type=jnp.float32)
        top_s, top_i = lax.top_k(s, PEER_TOPK)
        cand_s = (top_s[:, :, 0, :, None] + top_s[:, :, 1, None, :]).reshape(PEER_BLOCK, PEER_HEADS, PEER_TOPK * PEER_TOPK)
        cand_i = (top_i[:, :, 0, :, None] * PEER_KEYS + top_i[:, :, 1, None, :]).reshape(PEER_BLOCK, PEER_HEADS, PEER_TOPK * PEER_TOPK)
        best_s, best_pos = lax.top_k(cand_s, PEER_TOPK)
        expert = jnp.take_along_axis(cand_i, best_pos, axis=-1)
        gate = jax.nn.softmax(best_s, axis=-1)
        u = peer_u[expert]
        act = jax.nn.gelu(jnp.einsum('thkd,td->thk', u, xb, preferred_element_type=jnp.float32), approximate=False)
        return jnp.einsum('thk,thkd->td', (gate * act).astype(peer_v.dtype), peer_v[expert])

    y = lax.map(one_block, xf).reshape(n_blocks * PEER_BLOCK, D)[:N]
    return y.reshape(B, T, D).astype(xn.dtype)


def decoder_layer(x, past, g_attn, w_in, b_forget, g_q_nope, g_q_pe, g_ckv, g_k_pe, w_ukv, g_k_nope,
                  g_fox_q, g_fox_k, g_out_mla, g_out_fox, w_o, g_ffn, w_peer_q, peer_sub_keys, peer_u, peer_v):
    mix, rows = token_mixers(rmsnorm(x, g_attn), past, w_in, b_forget, g_q_nope, g_q_pe, g_ckv, g_k_pe,
                             w_ukv, g_k_nope, g_fox_q, g_fox_k, g_out_mla, g_out_fox, w_o)
    h = x + mix
    y = h + peer_ffn(rmsnorm(h, g_ffn), w_peer_q, peer_sub_keys, peer_u, peer_v)
    return y, rows


def setup_inputs(seed: int = 0) -> dict:
    key = jax.random.key(seed)
    k = jax.random.split(key, 32)
    f32 = jnp.float32

    def nrm(i, shape, scale):
        return scale * jax.random.normal(k[i], shape, f32)

    def gain(i, shape):
        return 1.0 + 0.1 * jax.random.normal(k[i], shape, f32)

    n_pages = PAST_LEN // PAGE_SIZE
    n_used = DEC_BATCH * n_pages
    n_pool = n_used + n_used // 4
    page_table = jax.random.permutation(k[7], n_pool)[:n_used].reshape(DEC_BATCH, n_pages).astype(jnp.int32)
    pool = (DEPTH, n_pool, PAGE_SIZE)
    return {
        'x_prompt': nrm(0, (BATCH, SEQ, D_MODEL), 1.0),
        'x_sample': nrm(1, (DEC_BATCH, DEC_SEQ, D_MODEL), 1.0),
        'cache_mla_latent': nrm(2, pool + (KV_RANK,), 1.0),
        'cache_mla_kpe': nrm(3, pool + (MLA_ROPE,), 1.0),
        'cache_fox_k': nrm(4, pool + (FOX_HEADS, FOX_DIM), 1.0),
        'cache_fox_v': nrm(5, pool + (FOX_HEADS, FOX_DIM), 1.0),
        'cache_fox_logf': jax.nn.log_sigmoid(3.0 + jax.random.normal(k[6], pool + (FOX_HEADS,), f32)),
        'page_table': page_table,
        'g_attn': gain(8, (DEPTH, D_MODEL)),
        'w_in': nrm(9, (DEPTH, D_MODEL, D_IN_PROJ), D_MODEL ** -0.5),
        'b_forget': jnp.linspace(1.0, 6.0, FOX_HEADS, dtype=f32)[None, :] + nrm(10, (DEPTH, FOX_HEADS), 0.1),
        'g_q_nope': gain(11, (DEPTH, MLA_NOPE)),
        'g_q_pe': gain(12, (DEPTH, MLA_ROPE)),
        'g_ckv': gain(13, (DEPTH, KV_RANK)),
        'g_k_pe': gain(14, (DEPTH, MLA_ROPE)),
        'w_ukv': nrm(15, (DEPTH, KV_RANK, MLA_HEADS * (MLA_NOPE + MLA_V)), KV_RANK ** -0.5),
        'g_k_nope': gain(16, (DEPTH, MLA_NOPE)),
        'g_fox_q': gain(17, (DEPTH, FOX_DIM)),
        'g_fox_k': gain(18, (DEPTH, FOX_DIM)),
        'g_out_mla': gain(19, (DEPTH, MLA_HEADS * MLA_V)),
        'g_out_fox': gain(20, (DEPTH, FOX_HEADS * FOX_DIM)),
        'w_o': nrm(21, (DEPTH, D_MIX, D_MODEL), D_MIX ** -0.5),
        'g_ffn': gain(22, (DEPTH, D_MODEL)),
        'w_peer_q': nrm(23, (DEPTH, D_MODEL, PEER_HEADS * PEER_DKEY), D_MODEL ** -0.5),
        'peer_sub_keys': nrm(24, (DEPTH, PEER_HEADS, 2, PEER_KEYS, PEER_DKEY // 2), (PEER_DKEY // 2) ** -0.5),
        'peer_u': nrm(25, (DEPTH, PEER_EXPERTS, D_MODEL), D_MODEL ** -0.5),
        'peer_v': nrm(26, (DEPTH, PEER_EXPERTS, D_MODEL), PEER_HEADS ** -0.5),
    }


def reference(x_prompt, x_sample, cache_mla_latent, cache_mla_kpe, cache_fox_k, cache_fox_v, cache_fox_logf,
              page_table, g_attn, w_in, b_forget, g_q_nope, g_q_pe, g_ckv, g_k_pe, w_ukv, g_k_nope,
              g_fox_q, g_fox_k, g_out_mla, g_out_fox, w_o, g_ffn, w_peer_q, peer_sub_keys, peer_u, peer_v):
    def gather(cache):
        g = cache[page_table]
        return g.reshape((g.shape[0], g.shape[1] * g.shape[2]) + g.shape[3:])

    y_prompt, y_sample = x_prompt, x_sample
    prompt_rows, sample_rows = [], []
    for l in range(DEPTH):
        lw = (g_attn[l], w_in[l], b_forget[l], g_q_nope[l], g_q_pe[l], g_ckv[l], g_k_pe[l], w_ukv[l],
              g_k_nope[l], g_fox_q[l], g_fox_k[l], g_out_mla[l], g_out_fox[l], w_o[l], g_ffn[l],
              w_peer_q[l], peer_sub_keys[l], peer_u[l], peer_v[l])
        y_prompt, rows_p = decoder_layer(y_prompt, None, *lw)
        past = (gather(cache_mla_latent[l]), gather(cache_mla_kpe[l]), gather(cache_fox_k[l]),
                gather(cache_fox_v[l]), gather(cache_fox_logf[l]))
        y_sample, rows_s = decoder_layer(y_sample, past, *lw)
        prompt_rows.append(rows_p)
        sample_rows.append(rows_s)
    lat_p, kpe_p, fk_p, fv_p, lf_p = [jnp.stack([r[i] for r in prompt_rows]) for i in range(5)]
    lat_s, kpe_s, fk_s, fv_s, lf_s = [jnp.stack([r[i] for r in sample_rows]) for i in range(5)]
    return (y_prompt, y_sample, lat_p, kpe_p, fk_p, fv_p, lf_p, lat_s, kpe_s, fk_s, fv_s, lf_s)
```

```python
import functools
import math

import numpy as np
import jax
import jax.numpy as jnp
from jax import lax
from jax.experimental import pallas as pl
from jax.experimental.pallas import tpu as pltpu

F32 = jnp.float32
BF16 = jnp.bfloat16

MLA_HEADS, MLA_NOPE, MLA_ROPE, MLA_V, KV_RANK = 8, 64, 32, 64, 128
ROPE_THETA = 10000.0
FOX_HEADS, FOX_DIM = 8, 64
PEER_HEADS, PEER_KEYS, PEER_DKEY, PEER_TOPK = 8, 128, 256, 16
NORM_EPS = 1e-6
NEG_INF = -1e30
SQRT_HALF = 0.7071067811865476

LANES = 128
VMEM_LIMIT = 56 * 1024 * 1024

C_QN, C_QPE, C_QPES, C_CKV, C_KPE, C_KPES, C_FQ, C_FK, C_FV, C_FL, C_END = (
    0, 512, 768, 1024, 1152, 1280, 1408, 1920, 2432, 2944, 3072)


def _nt(a, b):
    return lax.dot_general(a, b, (((1,), (1,)), ((), ())), preferred_element_type=F32)


def _mm(a, b):
    return jnp.dot(a, b, preferred_element_type=F32)


def _rms_rows(x, g):
    return x * lax.rsqrt(jnp.mean(x * x, axis=-1, keepdims=True) + NORM_EPS) * g


def _split2(x):
    hi = x.astype(BF16)
    lo = (x - hi.astype(F32)).astype(BF16)
    return hi, lo


def _group_rsqrt(y, m_ref):
    c = m_ref.shape[0]
    outs = []
    for a in range(0, y.shape[1], c):
        ya = y[:, a:a + c]
        hi, lo = _split2(ya * ya)
        ms = _mm(hi, m_ref[...]) + _mm(lo, m_ref[...])
        outs.append(lax.rsqrt(ms + NORM_EPS))
    return outs[0] if len(outs) == 1 else jnp.concatenate(outs, axis=1)


def _proj_body(x_ref, ga_ref, w_ref, wukv_ref, cq_ref, sq_ref, ck_ref, sk_ref,
               gqn_ref, gqp_ref, gqps_ref, gckv_ref, gkp_ref, gkps_ref, gfq_ref, gfk_ref, gkn_ref, bf_ref,
               m64_ref, m32_ref, m32s_ref,
               qn_o, qpe_o, kn_o, kpe4_o, vm_o, fq_o, fk_o, fv_o, ckv_o, kpe_o, fk32_o, fv32_o, logf_o):
    xn = _rms_rows(x_ref[...], ga_ref[...]).astype(BF16)

    def proj(a, b):
        return _mm(xn, w_ref[:, a:b])

    y = proj(C_QN, C_QPE)
    qn_o[...] = (y * _group_rsqrt(y, m64_ref) * gqn_ref[...]).astype(BF16)

    a = proj(C_QPE, C_QPES)
    b = proj(C_QPES, C_CKV)
    r = _group_rsqrt(a, m32_ref)
    qpe_o[...] = (r * (a * gqp_ref[...] * cq_ref[...] + b * gqps_ref[...] * sq_ref[...])).astype(BF16)

    ckv = _rms_rows(proj(C_CKV, C_KPE), gckv_ref[...])
    ckv_o[...] = ckv

    a = proj(C_KPE, C_KPES)
    b = proj(C_KPES, C_FQ)
    r = _group_rsqrt(a, m32s_ref)
    kpe4 = r * (a * gkp_ref[...] * ck_ref[...] + b * gkps_ref[...] * sk_ref[...])
    kpe4_o[...] = kpe4.astype(BF16)
    kpe_o[...] = kpe4[:, :MLA_ROPE]

    y = proj(C_FQ, C_FK)
    fq_o[...] = (y * _group_rsqrt(y, m64_ref) * gfq_ref[...]).astype(BF16)

    y = proj(C_FK, C_FV)
    fk = y * _group_rsqrt(y, m64_ref) * gfk_ref[...]
    fk32_o[...] = fk
    fk_o[...] = fk.astype(BF16)

    y = proj(C_FV, C_FL)
    fv32_o[...] = y
    fv_o[...] = y.astype(BF16)

    z = proj(C_FL, C_END) + bf_ref[...]
    logf = jnp.minimum(z, 0.0) - jnp.log1p(jnp.exp(-jnp.abs(z)))
    logf_o[...] = logf[:, :FOX_HEADS]

    kv = _mm(ckv.astype(BF16), wukv_ref[...])
    kn = kv[:, :MLA_HEADS * MLA_NOPE]
    kn_o[...] = (kn * _group_rsqrt(kn, m64_ref) * gkn_ref[...]).astype(BF16)
    vm_o[...] = kv[:, MLA_HEADS * MLA_NOPE:].astype(BF16)


def _proj_call(x, tm, n_pos_blocks, consts, tabs):
    n, d = x.shape
    cq, sq, ck, sk = tabs
    full = lambda a: pl.BlockSpec(a.shape, lambda i: (0,) * a.ndim)
    row = lambda w: pl.BlockSpec((tm, w), lambda i: (i, 0))
    tab = lambda w: pl.BlockSpec((tm, w), lambda i: (i % n_pos_blocks, 0))
    in_specs = ([row(d), full(consts[0]), full(consts[1]), full(consts[2]), tab(256), tab(256), tab(128), tab(128)]
                + [full(c) for c in consts[3:]])
    widths = [(512, BF16), (256, BF16), (512, BF16), (128, BF16), (512, BF16), (512, BF16), (512, BF16), (512, BF16),
              (KV_RANK, F32), (MLA_ROPE, F32), (512, F32), (512, F32), (FOX_HEADS, F32)]
    return pl.pallas_call(
        _proj_body,
        grid=(n // tm,),
        in_specs=in_specs,
        out_specs=[row(w) for w, _ in widths],
        out_shape=[jax.ShapeDtypeStruct((n, w), dt) for w, dt in widths],
        compiler_params=pltpu.CompilerParams(dimension_semantics=("parallel",), vmem_limit_bytes=VMEM_LIMIT),
        name="proj",
    )(x, consts[0], consts[1], consts[2], cq, sq, ck, sk, *consts[3:])


def _cumsum_body(x_ref, tri_ref, o_ref):
    t = x_ref.shape[1]
    blk = tri_ref.shape[0]
    carry = jnp.zeros((x_ref.shape[0], 1), F32)
    for a in range(0, t, blk):
        x = x_ref[:, a:a + blk]
        hi, lo = _split2(x)
        lo2 = (x - hi.astype(F32) - lo.astype(F32)).astype(BF16)
        c = _mm(hi, tri_ref[...]) + _mm(lo, tri_ref[...]) + _mm(lo2, tri_ref[...]) + carry
        o_ref[:, a:a + blk] = c
        carry = c[:, blk - 1:blk]


def _cumsum_call(logf_t, tri):
    b, h, t = logf_t.shape
    return pl.pallas_call(
        _cumsum_body,
        grid=(b,),
        in_specs=[pl.BlockSpec((None, h, t), lambda i: (i, 0, 0)), pl.BlockSpec(tri.shape, lambda i: (0, 0))],
        out_specs=pl.BlockSpec((None, h, t), lambda i: (i, 0, 0)),
        out_shape=jax.ShapeDtypeStruct((b, h, t), F32),
        compiler_params=pltpu.CompilerParams(dimension_semantics=("parallel",)),
        name="cumsum",
    )(logf_t, tri)


def _softmax_step(s, m, l, acc, v):
    m_new = jnp.maximum(m, jnp.max(s, axis=-1, keepdims=True))
    alpha = jnp.exp(m - m_new)
    p = jnp.exp(s - m_new)
    l = l * alpha + jnp.sum(p, axis=-1, keepdims=True)
    acc = acc * alpha + _mm(p.astype(BF16), v)
    return m_new, l, acc


def _attn_body(qn_ref, qpe_ref, kn_ref, kpe4_ref, vm_ref, fq_ref, fk_ref, fv_ref, ccol_ref, crow_ref,
               om_o, of_o, *, tq):
    hp = pl.program_id(1)
    qi = pl.program_id(2)
    lane_q = lax.broadcasted_iota(jnp.int32, (tq, 2 * LANES), 1)
    lane_o = lax.broadcasted_iota(jnp.int32, (tq, LANES), 1)
    q_m = jnp.concatenate([qn_ref[...], qpe_ref[...]], axis=1).astype(F32)
    q_f = fq_ref[...].astype(F32)
    qm, qf, cq = [], [], []
    for hh in range(2):
        quad = 2 * (hp % 2) + hh
        want = jnp.where(lane_q < LANES, hh * MLA_NOPE, LANES + quad * MLA_ROPE)
        start = jnp.where(lane_q < LANES, (lane_q // MLA_NOPE) * MLA_NOPE, (lane_q // MLA_ROPE) * MLA_ROPE)
        qm.append(jnp.where(start == want, q_m, 0.0).astype(BF16))
        qf.append(jnp.where(lane_o // FOX_DIM == hh, q_f, 0.0).astype(BF16))
        cq.append(ccol_ref[:, hh:hh + 1])

    row_id = lax.broadcasted_iota(jnp.int32, (tq, tq), 0)
    col_id = lax.broadcasted_iota(jnp.int32, (tq, tq), 1)
    causal = col_id <= row_id

    def block(j, carry, masked):
        off = pl.multiple_of(j * tq, tq)
        k_m = jnp.concatenate([kn_ref[pl.ds(off, tq), :], kpe4_ref[pl.ds(off, tq), :]], axis=1)
        v_m = vm_ref[pl.ds(off, tq), :]
        k_f = fk_ref[pl.ds(off, tq), :]
        v_f = fv_ref[pl.ds(off, tq), :]
        out = []
        for hh in range(2):
            s = _nt(qm[hh], k_m)
            if masked:
                s = jnp.where(causal, s, NEG_INF)
            out.append(_softmax_step(s, *carry[hh], v_m))
        for hh in range(2):
            s = _nt(qf[hh], k_f) + cq[hh] - crow_ref[hh:hh + 1, pl.ds(off, tq)]
            if masked:
                s = jnp.where(causal, s, NEG_INF)
            out.append(_softmax_step(s, *carry[2 + hh], v_f))
        return tuple(out)

    init = tuple((jnp.full((tq, 1), NEG_INF, F32), jnp.zeros((tq, 1), F32), jnp.zeros((tq, LANES), F32))
                 for _ in range(4))
    carry = lax.fori_loop(0, qi, lambda j, c: block(j, c, False), init)
    carry = block(qi, carry, True)
    first = lane_o < MLA_V
    om_o[...] = jnp.where(first, carry[0][2] / carry[0][1], carry[1][2] / carry[1][1])
    of_o[...] = jnp.where(first, carry[2][2] / carry[2][1], carry[3][2] / carry[3][1])


def _attn_call(qn, qpe, kn, kpe4, vm, fq, fk, fv, ccol, crow, b, t, tq):
    n = b * t
    nq = t // tq
    qspec = lambda cmap: pl.BlockSpec((tq, LANES), lambda bi, hp, qi: (bi * nq + qi, cmap(hp)))
    kspec = lambda cmap: pl.BlockSpec((t, LANES), lambda bi, hp, qi: (bi, cmap(hp)))
    pair = lambda hp: hp
    in_specs = [qspec(pair), qspec(lambda hp: hp // 2), kspec(pair), kspec(lambda hp: 0), kspec(pair),
                qspec(pair), kspec(pair), kspec(pair),
                pl.BlockSpec((None, tq, 2), lambda bi, hp, qi: (hp, bi * nq + qi, 0)),
                pl.BlockSpec((None, 2, t), lambda bi, hp, qi: (bi * 4 + hp, 0, 0))]
    ospec = pl.BlockSpec((tq, LANES), lambda bi, hp, qi: (bi * nq + qi, hp))
    return pl.pallas_call(
        functools.partial(_attn_body, tq=tq),
        grid=(b, 4, nq),
        in_specs=in_specs,
        out_specs=[ospec, ospec],
        out_shape=[jax.ShapeDtypeStruct((n, 512), F32)] * 2,
        compiler_params=pltpu.CompilerParams(dimension_semantics=("parallel", "parallel", "arbitrary"),
                                             vmem_limit_bytes=VMEM_LIMIT),
        name="attn",
    )(qn, qpe, kn, kpe4, vm, fq, fk, fv, ccol, crow)


def _suffix_incl(x):
    n = x.shape[1]
    lane = lax.broadcasted_iota(jnp.int32, (1, n), 1)
    d = 1
    while d < n:
        x = x + jnp.where(lane + d < n, pltpu.roll(x, n - d, 1), 0.0)
        d *= 2
    return x


def _decode_body(pt_ref, qn_ref, qpe_ref, qf_ref, latn_ref, kpen_ref, fkn_ref, fvn_ref, logfn_ref,
                 wk_ref, wv_ref, e64t_ref, gkn_ref, *rest, n_slots):
    del pt_ref
    pages = rest[:5 * n_slots]
    om_o, of_o = rest[5 * n_slots:5 * n_slots + 2]
    qabs_s, mm_s, lm_s, am_s, mf_s, lf_s, af_s, cr_s = rest[5 * n_slots + 2:]
    step = pl.program_id(1)
    qpe = qpe_ref[0]
    qf = qf_ref[0]

    def mla_scores(lat, kpe):
        latb = lat.astype(BF16)
        kraw = _mm(latb, wk_ref[...])
        ssq = _nt(e64t_ref[...], (kraw * kraw).astype(BF16))
        s = _nt(qabs_s[...].astype(BF16), latb) * lax.rsqrt(ssq + NORM_EPS) + _nt(qpe, kpe.astype(BF16))
        return s, latb

    @pl.when(step == 0)
    def _():
        qt = (qn_ref[0].astype(F32) * gkn_ref[...]).astype(BF16)
        qabs_s[...] = _nt(qt, wk_ref[...])
        rows = qabs_s.shape[0]
        s, latb = mla_scores(jnp.broadcast_to(latn_ref[0], (rows, KV_RANK)),
                             jnp.broadcast_to(kpen_ref[0], (rows, MLA_ROPE)))
        mm_s[...] = s[:, 0:1]
        lm_s[...] = jnp.ones_like(lm_s)
        am_s[...] = latb.astype(F32)
        fk8 = jnp.broadcast_to(fkn_ref[0], (rows, fkn_ref.shape[2])).astype(BF16)
        mf_s[...] = _nt(qf, fk8)[:, 0:1]
        lf_s[...] = jnp.ones_like(lf_s)
        af_s[...] = jnp.broadcast_to(fvn_ref[0], af_s.shape).astype(BF16).astype(F32)
        cr_s[...] = logfn_ref[0]

    def update(s, m_s, l_s, a_s, v):
        m_old = m_s[...]
        m_new = jnp.maximum(m_old, jnp.max(s, axis=-1, keepdims=True))
        alpha = jnp.exp(m_old - m_new)
        p = jnp.exp(s - m_new)
        l_s[...] = l_s[...] * alpha + jnp.sum(p, axis=-1, keepdims=True)
        a_s[...] = a_s[...] * alpha + _mm(p.astype(BF16), v)
        m_s[...] = m_new

    for g in range(n_slots):
        lat_ref, kpe_ref, fk_ref, fv_ref, lf_ref = pages[5 * g:5 * g + 5]
        s, latb = mla_scores(lat_ref[0], kpe_ref[0])
        update(s, mm_s, lm_s, am_s, latb)
        x = lf_ref[0]
        incl = _suffix_incl(x)
        carry = cr_s[...]
        s = _nt(qf, fk_ref[0].astype(BF16)) + (incl - x + carry)
        cr_s[...] = carry + incl[:, 0:1]
        update(s, mf_s, lf_s, af_s, fv_ref[0].astype(BF16))

    @pl.when(step == pl.num_programs(1) - 1)
    def _():
        hi, lo = _split2(am_s[...] / lm_s[...])
        ov = _mm(hi, wv_ref[...]) + _mm(lo, wv_ref[...])
        rows, width = ov.shape
        diag = (lax.broadcasted_iota(jnp.int32, (rows, width), 1) // MLA_V
                == lax.broadcasted_iota(jnp.int32, (rows, width), 0))
        om_o[0] = jnp.sum(jnp.where(diag, ov, 0.0), axis=0, keepdims=True)
        of_o[0] = jnp.sum(jnp.where(diag, af_s[...] / lf_s[...], 0.0), axis=0, keepdims=True)


def _decode_call(page_table, qnblk, qpeblk, qfblk, latn, kpen, fkn, fvn, logfn, wk, wv, e64t, gkn,
                 c_lat, c_kpe, c_fk, c_fv, c_lft, n_slots):
    db, n_pages = page_table.shape
    steps = n_pages // n_slots
    seq = lambda a: pl.BlockSpec((1,) + a.shape[1:], lambda b, s, pt: (b,) + (0,) * (a.ndim - 1))
    full = lambda a: pl.BlockSpec(a.shape, lambda b, s, pt: (0,) * a.ndim)

    def page(a, g):
        return pl.BlockSpec((1,) + a.shape[1:],
                            lambda b, s, pt: (pt[b, n_pages - 1 - (s * n_slots + g)],) + (0,) * (a.ndim - 1))

    caches = (c_lat, c_kpe, c_fk, c_fv, c_lft)
    in_specs = ([seq(a) for a in (qnblk, qpeblk, qfblk, latn, kpen, fkn, fvn, logfn)]
                + [full(a) for a in (wk, wv, e64t, gkn)]
                + [page(a, g) for g in range(n_slots) for a in caches])
    ospec = pl.BlockSpec((1, 1, 512), lambda b, s, pt: (b, 0, 0))
    h = MLA_HEADS
    scratch = [pltpu.VMEM((h, KV_RANK), F32), pltpu.VMEM((h, 1), F32), pltpu.VMEM((h, 1), F32),
               pltpu.VMEM((h, KV_RANK), F32), pltpu.VMEM((h, 1), F32), pltpu.VMEM((h, 1), F32),
               pltpu.VMEM((h, 512), F32), pltpu.VMEM((h, 1), F32)]
    return pl.pallas_call(
        functools.partial(_decode_body, n_slots=n_slots),
        grid_spec=pltpu.PrefetchScalarGridSpec(
            num_scalar_prefetch=1, grid=(db, steps), in_specs=in_specs, out_specs=[ospec, ospec],
            scratch_shapes=scratch),
        out_shape=[jax.ShapeDtypeStruct((db, 1, 512), F32)] * 2,
        compiler_params=pltpu.CompilerParams(dimension_semantics=("parallel", "arbitrary"),
                                             vmem_limit_bytes=VMEM_LIMIT),
        name="decode",
    )(page_table, qnblk, qpeblk, qfblk, latn, kpen, fkn, fvn, logfn, wk, wv, e64t, gkn,
      *[a for _ in range(n_slots) for a in caches])


def _outproj_body(om_ref, of_ref, x_ref, gm_ref, gf_ref, wo1_ref, wo2_ref, gffn_ref, h_o, xn2_o):
    mm = _rms_rows(om_ref[...], gm_ref[...]).astype(BF16)
    mf = _rms_rows(of_ref[...], gf_ref[...]).astype(BF16)
    h = x_ref[...] + _mm(mm, wo1_ref[...]) + _mm(mf, wo2_ref[...])
    h_o[...] = h
    xn2_o[...] = _rms_rows(h, gffn_ref[...]).astype(BF16)


def _outproj_call(om, of, x, gm, gf, wo1, wo2, gffn, tm):
    n, d = x.shape
    full = lambda a: pl.BlockSpec(a.shape, lambda i: (0,) * a.ndim)
    row = lambda w: pl.BlockSpec((tm, w), lambda i: (i, 0))
    return pl.pallas_call(
        _outproj_body,
        grid=(n // tm,),
        in_specs=[row(512), row(512), row(d)] + [full(a) for a in (gm, gf, wo1, wo2, gffn)],
        out_specs=[row(d), row(d)],
        out_shape=[jax.ShapeDtypeStruct((n, d), F32), jax.ShapeDtypeStruct((n, d), BF16)],
        compiler_params=pltpu.CompilerParams(dimension_semantics=("parallel",), vmem_limit_bytes=VMEM_LIMIT),
        name="outproj",
    )(om, of, x, gm, gf, wo1, wo2, gffn)


def _extract_max(s, iota_f):
    mx = jnp.max(s, axis=0, keepdims=True)
    idx = jnp.min(jnp.where(s == mx, iota_f, float(s.shape[0])), axis=0, keepdims=True)
    return mx, idx, iota_f == idx


def _top_rows(s, k):
    iota_f = lax.broadcasted_iota(jnp.int32, s.shape, 0).astype(F32)
    slot = lax.broadcasted_iota(jnp.int32, (k, s.shape[1]), 0)
    rank = jnp.full(s.shape, float(k), F32)
    vals = jnp.zeros((k, s.shape[1]), F32)
    for r in range(k):
        mx, _, hit = _extract_max(s, iota_f)
        rank = jnp.where(hit, float(r), rank)
        s = jnp.where(hit, -jnp.inf, s)
        vals = jnp.where(slot == r, mx, vals)
    return vals, rank


def _route_body(xn_ref, wqt_ref, keys_ref, r1_o, n0_o, e0_o, e1_o, s_s):
    k = PEER_TOPK
    tt = xn_ref.shape[0]
    qt = _nt(wqt_ref[...], xn_ref[...]).astype(BF16)
    half = PEER_DKEY // 2
    for hp in range(2 * PEER_HEADS):
        s_s[hp] = _mm(keys_ref[hp], qt[hp * half:(hp + 1) * half, :])

    def body(it, _):
        h = it // (tt // LANES)
        col = pl.ds(pl.multiple_of((it % (tt // LANES)) * LANES, LANES), LANES)
        s0 = s_s[2 * h, :, col]
        s1 = s_s[2 * h + 1, :, col]
        v0, rank0 = _top_rows(s0, k)
        v1, rank1 = _top_rows(s1, k)
        cand = jnp.concatenate([v0[a:a + 1, :] + v1 for a in range(k)], axis=0)
        iota_f = lax.broadcasted_iota(jnp.int32, cand.shape, 0).astype(F32)
        n0 = jnp.zeros(rank0.shape, F32)
        z = jnp.zeros((1, LANES), F32)
        top = v0[0:1, :] + v1[0:1, :]
        for _r in range(k):
            mx, idx, hit = _extract_max(cand, iota_f)
            a_star = jnp.floor(idx * (1.0 / k))
            n0 = n0 + jnp.where(rank0 == a_star, 1.0, 0.0)
            z = z + jnp.exp(mx - top)
            cand = jnp.where(hit, -jnp.inf, cand)
        r1_o[h, :, col] = rank1
        n0_o[h, :, col] = n0
        e0_o[h, :, col] = jnp.exp(s0 - v0[0:1, :]) / z
        e1_o[h, :, col] = jnp.exp(s1 - v1[0:1, :])
        return 0

    lax.fori_loop(0, PEER_HEADS * (tt // LANES), body, 0)


def _route_call(xn2, wqt, keys, tt):
    n = xn2.shape[0]
    full = lambda a: pl.BlockSpec(a.shape, lambda i: (0,) * a.ndim)
    ospec = pl.BlockSpec((PEER_HEADS, PEER_KEYS, tt), lambda i: (0, 0, i))
    oshape = jax.ShapeDtypeStruct((PEER_HEADS, PEER_KEYS, n), F32)
    return pl.pallas_call(
        _route_body,
        grid=(n // tt,),
        in_specs=[pl.BlockSpec((tt, xn2.shape[1]), lambda i: (i, 0)), full(wqt), full(keys)],
        out_specs=[ospec] * 4,
        out_shape=[oshape] * 4,
        scratch_shapes=[pltpu.VMEM((2 * PEER_HEADS, PEER_KEYS, tt), F32)],
        compiler_params=pltpu.CompilerParams(dimension_semantics=("parallel",), vmem_limit_bytes=VMEM_LIMIT),
        name="route",
    )(xn2, wqt, keys)


def _experts_body(xn_ref, h_ref, u_ref, vt_ref, r1_ref, n0_ref, e0_ref, e1_ref, y_o, act_s, p_s, acc_s):
    e = pl.program_id(1)
    et, tt = act_s.shape
    rows_per_step = et // PEER_KEYS
    nch = tt // LANES

    @pl.when(e == 0)
    def _():
        acc_s[...] = jnp.zeros_like(acc_s)

    act_s[...] = _nt(u_ref[...], xn_ref[...])

    first_rows = pl.ds(pl.multiple_of(e * rows_per_step, rows_per_step), rows_per_step)

    def body(c, _):
        col = pl.ds(pl.multiple_of(c * LANES, LANES), LANES)
        n0 = [n0_ref[h, first_rows, col] for h in range(PEER_HEADS)]
        e0 = [e0_ref[h, first_rows, col] for h in range(PEER_HEADS)]
        for il in range(rows_per_step):
            g = jnp.zeros((PEER_KEYS, LANES), F32)
            for h in range(PEER_HEADS):
                sel = r1_ref[h, :, col] < n0[h][il:il + 1, :]
                g = g + jnp.where(sel, e1_ref[h, :, col] * e0[h][il:il + 1, :], 0.0)
            rows = slice(il * PEER_KEYS, (il + 1) * PEER_KEYS)
            a = act_s[rows, col]
            p_s[rows, col] = (0.5 * a * (1.0 + lax.erf(a * SQRT_HALF)) * g).astype(BF16)
        return 0

    lax.fori_loop(0, nch, body, 0)
    acc_s[...] += _mm(vt_ref[...], p_s[...])

    @pl.when(e == pl.num_programs(1) - 1)
    def _():
        y_o[...] = h_ref[...] + acc_s[...].T


def _experts_call(xn2, h, u, vt, r1, n0, e0, e1, tt, et):
    n, d = h.shape
    ne = u.shape[0]
    gate = pl.BlockSpec((PEER_HEADS, PEER_KEYS, tt), lambda t, e: (0, 0, t))
    return pl.pallas_call(
        _experts_body,
        grid=(n // tt, ne // et),
        in_specs=[pl.BlockSpec((tt, d), lambda t, e: (t, 0)), pl.BlockSpec((tt, d), lambda t, e: (t, 0)),
                  pl.BlockSpec((et, d), lambda t, e: (e, 0)), pl.BlockSpec((d, et), lambda t, e: (0, e)),
                  gate, gate, gate, gate],
        out_specs=pl.BlockSpec((tt, d), lambda t, e: (t, 0)),
        out_shape=jax.ShapeDtypeStruct((n, d), F32),
        scratch_shapes=[pltpu.VMEM((et, tt), F32), pltpu.VMEM((et, tt), BF16), pltpu.VMEM((d, tt), F32)],
        compiler_params=pltpu.CompilerParams(dimension_semantics=("parallel", "arbitrary"),
                                             vmem_limit_bytes=VMEM_LIMIT),
        name="experts",
    )(xn2, h, u, vt, r1, n0, e0, e1)


def _proj_columns():
    o_qn, o_qpe, o_ckv, o_kpe = 0, 512, 768, 896
    o_fq, o_fk, o_fv, o_fl = 928, 1440, 1952, 2464
    swap = lambda l: (l + MLA_ROPE // 2) % MLA_ROPE
    cols = list(range(o_qn, o_qn + 512))
    cols += list(range(o_qpe, o_qpe + 256))
    cols += [o_qpe + MLA_ROPE * h + swap(l) for h in range(MLA_HEADS) for l in range(MLA_ROPE)]
    cols += list(range(o_ckv, o_ckv + KV_RANK))
    cols += [o_kpe + l for _ in range(4) for l in range(MLA_ROPE)]
    cols += [o_kpe + swap(l) for _ in range(4) for l in range(MLA_ROPE)]
    cols += list(range(o_fq, o_fq + 512)) + list(range(o_fk, o_fk + 512)) + list(range(o_fv, o_fv + 512))
    cols += list(range(o_fl, o_fl + FOX_HEADS)) + [-1] * (LANES - FOX_HEADS)
    assert len(cols) == C_END
    return np.asarray(cols, np.int32)


def _block_diag_mean(size, group):
    idx = np.arange(size) // group
    return jnp.asarray((idx[:, None] == idx[None, :]).astype(np.float32) / group, BF16)


def _rope_tables(pos):
    half = MLA_ROPE // 2
    inv = jnp.power(ROPE_THETA, -jnp.arange(half, dtype=F32) / half)
    ang = pos.astype(F32)[:, None] * inv
    cos = jnp.concatenate([jnp.cos(ang), jnp.cos(ang)], axis=1)
    sin = jnp.concatenate([-jnp.sin(ang), jnp.sin(ang)], axis=1)
    return (jnp.tile(cos, (1, MLA_HEADS)), jnp.tile(sin, (1, MLA_HEADS)), jnp.tile(cos, (1, 4)), jnp.tile(sin, (1, 4)))


def _tile_for(n, cap):
    t = min(n, cap)
    while n % t:
        t //= 2
    return t


def kernel(x_prompt, x_sample, cache_mla_latent, cache_mla_kpe, cache_fox_k, cache_fox_v, cache_fox_logf, page_table, g_attn, w_in, b_forget, g_q_nope, g_q_pe, g_ckv, g_k_pe, w_ukv, g_k_nope, g_fox_q, g_fox_k, g_out_mla, g_out_fox, w_o, g_ffn, w_peer_q, peer_sub_keys, peer_u, peer_v):
    depth = g_attn.shape[0]
    assert depth == 1, "single-layer trunk"
    b, t, d = x_prompt.shape
    db, dt, _ = x_sample.shape
    assert dt == 1
    n_pages = page_table.shape[1]
    page = cache_mla_latent.shape[2]
    past = n_pages * page
    l = 0

    cols = _proj_columns()
    w_pad = jnp.concatenate([w_in[l], jnp.zeros((d, 1), F32)], axis=1)
    w_arr = w_pad[:, jnp.where(cols < 0, w_in.shape[2], cols)].astype(BF16)
    ukv = w_ukv[l].reshape(KV_RANK, MLA_HEADS, MLA_NOPE + MLA_V)
    wk = ukv[:, :, :MLA_NOPE].reshape(KV_RANK, MLA_HEADS * MLA_NOPE).astype(BF16)
    wv = ukv[:, :, MLA_NOPE:].reshape(KV_RANK, MLA_HEADS * MLA_V).astype(BF16)
    wukv = jnp.concatenate([wk, wv], axis=1)
    swap = (np.arange(MLA_ROPE) + MLA_ROPE // 2) % MLA_ROPE
    s_mla = (MLA_NOPE + MLA_ROPE) ** -0.5
    s_fox = FOX_DIM ** -0.5
    row = lambda v: v.reshape(1, -1).astype(F32)
    consts = [
        row(g_attn[l]), w_arr, wukv,
        row(jnp.tile(g_q_nope[l], MLA_HEADS) * s_mla), row(jnp.tile(g_q_pe[l], MLA_HEADS) * s_mla),
        row(jnp.tile(g_q_pe[l][swap], MLA_HEADS) * s_mla), row(g_ckv[l]),
        row(jnp.tile(g_k_pe[l], 4)), row(jnp.tile(g_k_pe[l][swap], 4)),
        row(jnp.tile(g_fox_q[l], FOX_HEADS) * s_fox), row(jnp.tile(g_fox_k[l], FOX_HEADS)),
        row(jnp.tile(g_k_nope[l], MLA_HEADS)),
        row(jnp.concatenate([b_forget[l], jnp.zeros((LANES - FOX_HEADS,), F32)])),
        _block_diag_mean(256, 64), _block_diag_mean(256, 32), _block_diag_mean(128, 32),
    ]

    n = b * t
    tm = _tile_for(t, 256)
    tabs_p = _rope_tables(jnp.arange(t, dtype=jnp.int32))
    xp = x_prompt.reshape(n, d)
    (qn, qpe, kn, kpe4, vm, fq, fk, fv, ckv_p, kpe_p, fk_p, fv_p, logf_p) = _proj_call(xp, tm, t // tm, consts, tabs_p)
    blk = _tile_for(t, 256)
    tri = jnp.asarray(np.triu(np.ones((blk, blk), np.float32)), BF16)
    c_t = _cumsum_call(logf_p.reshape(b, t, FOX_HEADS).transpose(0, 2, 1), tri)
    ccol = c_t.reshape(b, 4, 2, t).transpose(1, 0, 3, 2).reshape(4, n, 2)
    crow = c_t.reshape(b * 4, 2, t)
    tq = _tile_for(t, 256)
    om_p, of_p = _attn_call(qn, qpe, kn, kpe4, vm, fq, fk, fv, ccol, crow, b, t, tq)

    tabs_s = _rope_tables(jnp.full((db,), past, jnp.int32))
    xs = x_sample.reshape(db, d)
    (qn_s, qpe_s, _, _, _, fq_s, _, _, ckv_s, kpe_s, fk_s, fv_s, logf_s) = _proj_call(xs, db, 1, consts, tabs_s)
    head_of = np.arange(512) // MLA_NOPE
    blockmask = jnp.asarray(head_of[None, :] == np.arange(MLA_HEADS)[:, None])
    qnblk = jnp.where(blockmask[None], qn_s[:, None, :], jnp.zeros((), BF16))
    qfblk = jnp.where(blockmask[None], fq_s[:, None, :], jnp.zeros((), BF16))
    qpeblk = qpe_s.reshape(db, MLA_HEADS, MLA_ROPE)
    e64t = jnp.asarray(blockmask, BF16) * (1.0 / MLA_NOPE)
    n_slots = 8 if n_pages % 8 == 0 else 1
    om_s, of_s = _decode_call(
        page_table, qnblk, qpeblk, qfblk, ckv_s.reshape(db, 1, KV_RANK), kpe_s.reshape(db, 1, MLA_ROPE),
        fk_s.reshape(db, 1, 512), fv_s.reshape(db, 1, 512), logf_s.reshape(db, FOX_HEADS, 1),
        wk, wv, e64t.astype(BF16), consts[11],
        cache_mla_latent[l], cache_mla_kpe[l],
        cache_fox_k[l].reshape(-1, page, FOX_HEADS * FOX_DIM), cache_fox_v[l].reshape(-1, page, FOX_HEADS * FOX_DIM),
        cache_fox_logf[l].transpose(0, 2, 1), n_slots)

    wo1 = w_o[l][:512].astype(BF16)
    wo2 = w_o[l][512:].astype(BF16)
    gm, gf, gffn = row(g_out_mla[l]), row(g_out_fox[l]), row(g_ffn[l])
    wqt = w_peer_q[l].T.astype(BF16)
    keys = peer_sub_keys[l].reshape(2 * PEER_HEADS, PEER_KEYS, PEER_DKEY // 2).astype(BF16)
    u = peer_u[l].astype(BF16)
    vt = peer_v[l].T.astype(BF16)

    def tail(om, of, x, tm_o, tt, et):
        h, xn2 = _outproj_call(om, of, x, gm, gf, wo1, wo2, gffn, tm_o)
        pad = (-h.shape[0]) % tt
        if pad:
            h = jnp.pad(h, ((0, pad), (0, 0)))
            xn2 = jnp.pad(xn2, ((0, pad), (0, 0)))
        gates = _route_call(xn2, wqt, keys, tt)
        y = _experts_call(xn2, h, u, vt, *gates, tt, et)
        return y[:x.shape[0]]

    y_p = tail(om_p, of_p, xp, tm, _tile_for(n, 256), 1024)
    y_s = tail(om_s.reshape(db, 512), of_s.reshape(db, 512), xs, db, LANES, 1024)

    shp = lambda a, *s: a.reshape((depth,) + s)
    return (y_p.reshape(b, t, d), y_s.reshape(db, 1, d),
            shp(ckv_p, b, t, KV_RANK), shp(kpe_p, b, t, MLA_ROPE),
            shp(fk_p, b, t, FOX_HEADS, FOX_DIM), shp(fv_p, b, t, FOX_HEADS, FOX_DIM), shp(logf_p, b, t, FOX_HEADS),
            shp(ckv_s, db, 1, KV_RANK), shp(kpe_s, db, 1, MLA_ROPE),
            shp(fk_s, db, 1, FOX_HEADS, FOX_DIM), shp(fv_s, db, 1, FOX_HEADS, FOX_DIM), shp(logf_s, db, 1, FOX_HEADS))
```

```python
import functools
import math

import numpy as np
import jax
import jax.numpy as jnp
from jax import lax
from jax.experimental import pallas as pl
from jax.experimental.pallas import tpu as pltpu

F32 = jnp.float32
BF16 = jnp.bfloat16

MLA_HEADS, MLA_NOPE, MLA_ROPE, MLA_V, KV_RANK = 8, 64, 32, 64, 128
ROPE_THETA = 10000.0
FOX_HEADS, FOX_DIM = 8, 64
PEER_HEADS, PEER_KEYS, PEER_DKEY, PEER_TOPK = 8, 128, 256, 16
NORM_EPS = 1e-6
NEG_INF = -1e30
SQRT_HALF = 0.7071067811865476

LANES = 128
BF16_SUBLANES = 16
VMEM_LIMIT = 56 * 1024 * 1024

C_QN, C_QPE, C_QPES, C_CKV, C_KPE, C_KPES, C_FQ, C_FK, C_FV, C_FL, C_END = (
    0, 512, 768, 1024, 1152, 1280, 1408, 1920, 2432, 2944, 3072)


def _nt(a, b):
    return lax.dot_general(a, b, (((1,), (1,)), ((), ())), preferred_element_type=F32)


def _mm(a, b):
    return jnp.dot(a, b, preferred_element_type=F32)


def _rms_rows(x, g):
    return x * lax.rsqrt(jnp.mean(x * x, axis=-1, keepdims=True) + NORM_EPS) * g


def _split2(x):
    hi = x.astype(BF16)
    lo = (x - hi.astype(F32)).astype(BF16)
    return hi, lo


def _group_rsqrt(y, m_ref):
    c = m_ref.shape[0]
    outs = []
    for a in range(0, y.shape[1], c):
        ya = y[:, a:a + c]
        hi, lo = _split2(ya * ya)
        ms = _mm(hi, m_ref[...]) + _mm(lo, m_ref[...])
        outs.append(lax.rsqrt(ms + NORM_EPS))
    return outs[0] if len(outs) == 1 else jnp.concatenate(outs, axis=1)


def _proj_body(x_ref, ga_ref, w_ref, wukv_ref, cq_ref, sq_ref, ck_ref, sk_ref,
               gqn_ref, gqp_ref, gqps_ref, gckv_ref, gkp_ref, gkps_ref, gfq_ref, gfk_ref, gkn_ref, bf_ref,
               m64_ref, m32_ref, m32s_ref,
               qn_o, qpe_o, kn_o, kpe4_o, vm_o, fq_o, fk_o, fv_o, ckv_o, kpe_o, fk32_o, fv32_o, logf_o):
    xn = _rms_rows(x_ref[...], ga_ref[...]).astype(BF16)

    def proj(a, b):
        return _mm(xn, w_ref[:, a:b])

    y = proj(C_QN, C_QPE)
    qn_o[...] = (y * _group_rsqrt(y, m64_ref) * gqn_ref[...]).astype(BF16)

    a = proj(C_QPE, C_QPES)
    b = proj(C_QPES, C_CKV)
    r = _group_rsqrt(a, m32_ref)
    qpe_o[...] = (r * (a * gqp_ref[...] * cq_ref[...] + b * gqps_ref[...] * sq_ref[...])).astype(BF16)

    ckv = _rms_rows(proj(C_CKV, C_KPE), gckv_ref[...])
    ckv_o[...] = ckv

    a = proj(C_KPE, C_KPES)
    b = proj(C_KPES, C_FQ)
    r = _group_rsqrt(a, m32s_ref)
    kpe4 = r * (a * gkp_ref[...] * ck_ref[...] + b * gkps_ref[...] * sk_ref[...])
    kpe4_o[...] = kpe4.astype(BF16)
    kpe_o[...] = kpe4[:, :MLA_ROPE]

    y = proj(C_FQ, C_FK)
    fq_o[...] = (y * _group_rsqrt(y, m64_ref) * gfq_ref[...]).astype(BF16)

    y = proj(C_FK, C_FV)
    fk = y * _group_rsqrt(y, m64_ref) * gfk_ref[...]
    fk32_o[...] = fk
    fk_o[...] = fk.astype(BF16)

    y = proj(C_FV, C_FL)
    fv32_o[...] = y
    fv_o[...] = y.astype(BF16)

    z = proj(C_FL, C_END) + bf_ref[...]
    logf = jnp.minimum(z, 0.0) - jnp.log1p(jnp.exp(-jnp.abs(z)))
    logf_o[...] = logf[:, :FOX_HEADS]

    kv = _mm(ckv.astype(BF16), wukv_ref[...])
    kn = kv[:, :MLA_HEADS * MLA_NOPE]
    kn_o[...] = (kn * _group_rsqrt(kn, m64_ref) * gkn_ref[...]).astype(BF16)
    vm_o[...] = kv[:, MLA_HEADS * MLA_NOPE:].astype(BF16)


def _proj_call(x, tm, n_pos_blocks, consts, tabs):
    n, d = x.shape
    cq, sq, ck, sk = tabs
    full = lambda a: pl.BlockSpec(a.shape, lambda i: (0,) * a.ndim)
    row = lambda w: pl.BlockSpec((tm, w), lambda i: (i, 0))
    tab = lambda w: pl.BlockSpec((tm, w), lambda i: (i % n_pos_blocks, 0))
    in_specs = ([row(d), full(consts[0]), full(consts[1]), full(consts[2]), tab(256), tab(256), tab(128), tab(128)]
                + [full(c) for c in consts[3:]])
    widths = [(512, BF16), (256, BF16), (512, BF16), (128, BF16), (512, BF16), (512, BF16), (512, BF16), (512, BF16),
              (KV_RANK, F32), (MLA_ROPE, F32), (512, F32), (512, F32), (FOX_HEADS, F32)]
    return pl.pallas_call(
        _proj_body,
        grid=(n // tm,),
        in_specs=in_specs,
        out_specs=[row(w) for w, _ in widths],
        out_shape=[jax.ShapeDtypeStruct((n, w), dt) for w, dt in widths],
        compiler_params=pltpu.CompilerParams(dimension_semantics=("parallel",), vmem_limit_bytes=VMEM_LIMIT),
        name="proj",
    )(x, consts[0], consts[1], consts[2], cq, sq, ck, sk, *consts[3:])


def _cumsum_body(x_ref, tri_ref, o_ref):
    t = x_ref.shape[1]
    blk = tri_ref.shape[0]
    carry = jnp.zeros((x_ref.shape[0], 1), F32)
    for a in range(0, t, blk):
        x = x_ref[:, a:a + blk]
        hi, lo = _split2(x)
        lo2 = (x - hi.astype(F32) - lo.astype(F32)).astype(BF16)
        c = _mm(hi, tri_ref[...]) + _mm(lo, tri_ref[...]) + _mm(lo2, tri_ref[...]) + carry
        o_ref[:, a:a + blk] = c
        carry = c[:, blk - 1:blk]


def _cumsum_call(logf_t, tri):
    b, h, t = logf_t.shape
    return pl.pallas_call(
        _cumsum_body,
        grid=(b,),
        in_specs=[pl.BlockSpec((None, h, t), lambda i: (i, 0, 0)), pl.BlockSpec(tri.shape, lambda i: (0, 0))],
        out_specs=pl.BlockSpec((None, h, t), lambda i: (i, 0, 0)),
        out_shape=jax.ShapeDtypeStruct((b, h, t), F32),
        compiler_params=pltpu.CompilerParams(dimension_semantics=("parallel",)),
        name="cumsum",
    )(logf_t, tri)


def _softmax_step(s, m, l, acc, v):
    m_new = jnp.maximum(m, jnp.max(s, axis=-1, keepdims=True))
    alpha = jnp.exp(m - m_new)
    p = jnp.exp(s - m_new)
    l = l * alpha + jnp.sum(p, axis=-1, keepdims=True)
    acc = acc * alpha + _mm(p.astype(BF16), v)
    return m_new, l, acc


def _attn_body(qn_ref, qpe_ref, kn_ref, kpe4_ref, vm_ref, fq_ref, fk_ref, fv_ref, ccol_ref, crow_ref,
               om_o, of_o, *, tq):
    hp = pl.program_id(1)
    qi = pl.program_id(2)
    lane_q = lax.broadcasted_iota(jnp.int32, (tq, 2 * LANES), 1)
    lane_o = lax.broadcasted_iota(jnp.int32, (tq, LANES), 1)
    q_m = jnp.concatenate([qn_ref[...], qpe_ref[...]], axis=1).astype(F32)
    q_f = fq_ref[...].astype(F32)
    qm, qf, cq = [], [], []
    for hh in range(2):
        quad = 2 * (hp % 2) + hh
        want = jnp.where(lane_q < LANES, hh * MLA_NOPE, LANES + quad * MLA_ROPE)
        start = jnp.where(lane_q < LANES, (lane_q // MLA_NOPE) * MLA_NOPE, (lane_q // MLA_ROPE) * MLA_ROPE)
        qm.append(jnp.where(start == want, q_m, 0.0).astype(BF16))
        qf.append(jnp.where(lane_o // FOX_DIM == hh, q_f, 0.0).astype(BF16))
        cq.append(ccol_ref[:, hh:hh + 1])

    row_id = lax.broadcasted_iota(jnp.int32, (tq, tq), 0)
    col_id = lax.broadcasted_iota(jnp.int32, (tq, tq), 1)
    causal = col_id <= row_id

    def block(j, carry, masked):
        off = pl.multiple_of(j * tq, tq)
        k_m = jnp.concatenate([kn_ref[pl.ds(off, tq), :], kpe4_ref[pl.ds(off, tq), :]], axis=1)
        v_m = vm_ref[pl.ds(off, tq), :]
        k_f = fk_ref[pl.ds(off, tq), :]
        v_f = fv_ref[pl.ds(off, tq), :]
        out = []
        for hh in range(2):
            s = _nt(qm[hh], k_m)
            if masked:
                s = jnp.where(causal, s, NEG_INF)
            out.append(_softmax_step(s, *carry[hh], v_m))
        for hh in range(2):
            s = _nt(qf[hh], k_f) + cq[hh] - crow_ref[hh:hh + 1, pl.ds(off, tq)]
            if masked:
                s = jnp.where(causal, s, NEG_INF)
            out.append(_softmax_step(s, *carry[2 + hh], v_f))
        return tuple(out)

    init = tuple((jnp.full((tq, 1), NEG_INF, F32), jnp.zeros((tq, 1), F32), jnp.zeros((tq, LANES), F32))
                 for _ in range(4))
    carry = lax.fori_loop(0, qi, lambda j, c: block(j, c, False), init)
    carry = block(qi, carry, True)
    first = lane_o < MLA_V
    om_o[...] = jnp.where(first, carry[0][2] / carry[0][1], carry[1][2] / carry[1][1])
    of_o[...] = jnp.where(first, carry[2][2] / carry[2][1], carry[3][2] / carry[3][1])


def _attn_call(qn, qpe, kn, kpe4, vm, fq, fk, fv, ccol, crow, b, t, tq):
    n = b * t
    nq = t // tq
    qspec = lambda cmap: pl.BlockSpec((tq, LANES), lambda bi, hp, qi: (bi * nq + qi, cmap(hp)))
    kspec = lambda cmap: pl.BlockSpec((t, LANES), lambda bi, hp, qi: (bi, cmap(hp)))
    pair = lambda hp: hp
    in_specs = [qspec(pair), qspec(lambda hp: hp // 2), kspec(pair), kspec(lambda hp: 0), kspec(pair),
                qspec(pair), kspec(pair), kspec(pair),
                pl.BlockSpec((None, tq, 2), lambda bi, hp, qi: (hp, bi * nq + qi, 0)),
                pl.BlockSpec((None, 2, t), lambda bi, hp, qi: (bi * 4 + hp, 0, 0))]
    ospec = pl.BlockSpec((tq, LANES), lambda bi, hp, qi: (bi * nq + qi, hp))
    return pl.pallas_call(
        functools.partial(_attn_body, tq=tq),
        grid=(b, 4, nq),
        in_specs=in_specs,
        out_specs=[ospec, ospec],
        out_shape=[jax.ShapeDtypeStruct((n, 512), F32)] * 2,
        compiler_params=pltpu.CompilerParams(dimension_semantics=("parallel", "parallel", "arbitrary"),
                                             vmem_limit_bytes=VMEM_LIMIT),
        name="attn",
    )(qn, qpe, kn, kpe4, vm, fq, fk, fv, ccol, crow)


def _suffix_incl(x):
    n = x.shape[1]
    lane = lax.broadcasted_iota(jnp.int32, (1, n), 1)
    d = 1
    while d < n:
        x = x + jnp.where(lane + d < n, pltpu.roll(x, n - d, 1), 0.0)
        d *= 2
    return x


def _decode_body(pt_ref, qn_ref, qpe_ref, fqc_ref, latn_ref, kpen_ref, fkn_ref, fvn_ref, logfn_ref,
                 wkt_ref, wv_ref, gkn_ref, *rest, n_slots):
    del pt_ref
    pages = rest[:5 * n_slots]
    om_o, of_o = rest[5 * n_slots:5 * n_slots + 2]
    wkq_s, qfx_s, mm_s, lm_s, am_s, mf_s, lf_s, af_s, cr_s = rest[5 * n_slots + 2:]
    step = pl.program_id(1)
    qpe = qpe_ref[0]
    nh = MLA_HEADS
    kdim = MLA_HEADS * MLA_NOPE
    fdim = FOX_HEADS * FOX_DIM

    def head_sums(x):
        return x.reshape(x.shape[0] // FOX_DIM, FOX_DIM, x.shape[1]).sum(axis=1)

    def mla_scores(lat, kpe_t):
        latb = lat.astype(BF16)
        kq = _nt(wkq_s[...], latb)
        kraw_t = kq[:kdim]
        ssq = head_sums(kraw_t * kraw_t) * (1.0 / MLA_NOPE)
        s = kq[kdim:kdim + nh] * lax.rsqrt(ssq + NORM_EPS) + _mm(qpe, kpe_t.astype(BF16))
        return s, latb

    def fox_scores(fk_t):
        return head_sums(fk_t * qfx_s[...])

    @pl.when(step == 0)
    def _():
        qt = (qn_ref[0].astype(F32) * gkn_ref[...]).astype(BF16)
        qabs = _mm(qt, wkt_ref[...])
        wkq_s[0:kdim, :] = wkt_ref[...]
        wkq_s[kdim:kdim + 2 * nh, :] = jnp.concatenate([qabs, jnp.zeros_like(qabs)], axis=0).astype(BF16)
        qfx_s[...] = jnp.broadcast_to(fqc_ref[0], qfx_s.shape)
        s, latb = mla_scores(jnp.broadcast_to(latn_ref[0], (LANES, KV_RANK)),
                             jnp.broadcast_to(kpen_ref[0], (MLA_ROPE, LANES)))
        mm_s[...] = s[:, 0:1]
        lm_s[...] = jnp.ones_like(lm_s)
        am_s[...] = latb[0:nh].astype(F32)
        mf_s[...] = fox_scores(jnp.broadcast_to(fkn_ref[0], (fdim, LANES)))[:, 0:1]
        lf_s[...] = jnp.ones_like(lf_s)
        lane = lax.broadcasted_iota(jnp.int32, af_s.shape, 1)
        af_s[...] = jnp.where(lane == 0, jnp.broadcast_to(fvn_ref[0], af_s.shape), 0.0)
        cr_s[...] = logfn_ref[0]

    def softmax_update(s, m_s, l_s):
        m_old = m_s[...]
        m_new = jnp.maximum(m_old, jnp.max(s, axis=-1, keepdims=True))
        alpha = jnp.exp(m_old - m_new)
        p = jnp.exp(s - m_new)
        l_s[...] = l_s[...] * alpha + jnp.sum(p, axis=-1, keepdims=True)
        m_s[...] = m_new
        return p, alpha

    scores, latbs = zip(*[mla_scores(pages[5 * g][0], pages[5 * g + 1][0]) for g in range(n_slots)])
    p, alpha = softmax_update(jnp.concatenate(scores, axis=1), mm_s, lm_s)
    pv = _mm(p[:, 0:LANES].astype(BF16), latbs[0])
    for g in range(1, n_slots):
        pv = pv + _mm(p[:, g * LANES:(g + 1) * LANES].astype(BF16), latbs[g])
    am_s[...] = am_s[...] * alpha + pv

    carry = cr_s[...]
    scores = []
    for g in range(n_slots):
        x = pages[5 * g + 4][0]
        incl = _suffix_incl(x)
        scores.append(fox_scores(pages[5 * g + 2][0]) + (incl - x + carry))
        carry = carry + incl[:, 0:1]
    cr_s[...] = carry
    p, alpha = softmax_update(jnp.concatenate(scores, axis=1), mf_s, lf_s)
    for h in range(FOX_HEADS):
        rows = slice(h * FOX_DIM, (h + 1) * FOX_DIM)
        acc = af_s[rows, :] * alpha[h:h + 1, :]
        for g in range(n_slots):
            acc = acc + pages[5 * g + 3][0, rows, :] * p[h:h + 1, g * LANES:(g + 1) * LANES]
        af_s[rows, :] = acc

    @pl.when(step == pl.num_programs(1) - 1)
    def _():
        hi, lo = _split2(am_s[...] / lm_s[...])
        ov = _mm(hi, wv_ref[...]) + _mm(lo, wv_ref[...])
        rows, width = ov.shape
        diag = (lax.broadcasted_iota(jnp.int32, (rows, width), 1) // MLA_V
                == lax.broadcasted_iota(jnp.int32, (rows, width), 0))
        om_o[0] = jnp.sum(jnp.where(diag, ov, 0.0), axis=0, keepdims=True)
        linv = 1.0 / lf_s[...]
        scale = jnp.concatenate([jnp.broadcast_to(linv[h:h + 1, :], (FOX_DIM, 1)) for h in range(FOX_HEADS)], axis=0)
        of_o[0] = jnp.sum(af_s[...], axis=1, keepdims=True) * scale


def _decode_call(page_table, qnblk, qpeblk, fqc, latn, kpen, fkn, fvn, logfn, wkt, wv, gkn,
                 c_lat, c_kpe_t, c_fk_t, c_fv_t, c_lf_t, n_slots):
    db, n_pages = page_table.shape
    steps = n_pages // n_slots
    seq = lambda a: pl.BlockSpec((1,) + a.shape[1:], lambda b, s, pt: (b,) + (0,) * (a.ndim - 1))
    full = lambda a: pl.BlockSpec(a.shape, lambda b, s, pt: (0,) * a.ndim)

    def page(a, g):
        return pl.BlockSpec((1,) + a.shape[1:],
                            lambda b, s, pt: (pt[b, n_pages - 1 - (s * n_slots + g)],) + (0,) * (a.ndim - 1))

    caches = (c_lat, c_kpe_t, c_fk_t, c_fv_t, c_lf_t)
    in_specs = ([seq(a) for a in (qnblk, qpeblk, fqc, latn, kpen, fkn, fvn, logfn)]
                + [full(a) for a in (wkt, wv, gkn)]
                + [page(a, g) for g in range(n_slots) for a in caches])
    h = MLA_HEADS
    kdim = MLA_HEADS * MLA_NOPE
    fdim = FOX_HEADS * FOX_DIM
    col = lambda: pltpu.VMEM((h, 1), F32)
    scratch = [pltpu.VMEM((kdim + 2 * h, KV_RANK), BF16), pltpu.VMEM((fdim, LANES), F32),
               col(), col(), pltpu.VMEM((h, KV_RANK), F32), col(), col(), pltpu.VMEM((fdim, LANES), F32), col()]
    return pl.pallas_call(
        functools.partial(_decode_body, n_slots=n_slots),
        grid_spec=pltpu.PrefetchScalarGridSpec(
            num_scalar_prefetch=1, grid=(db, steps), in_specs=in_specs,
            out_specs=[pl.BlockSpec((1, 1, kdim), lambda b, s, pt: (b, 0, 0)),
                       pl.BlockSpec((1, fdim, 1), lambda b, s, pt: (b, 0, 0))],
            scratch_shapes=scratch),
        out_shape=[jax.ShapeDtypeStruct((db, 1, kdim), F32), jax.ShapeDtypeStruct((db, fdim, 1), F32)],
        compiler_params=pltpu.CompilerParams(dimension_semantics=("parallel", "arbitrary"),
                                             vmem_limit_bytes=VMEM_LIMIT),
        name="decode",
    )(page_table, qnblk, qpeblk, fqc, latn, kpen, fkn, fvn, logfn, wkt, wv, gkn,
      *[a for _ in range(n_slots) for a in caches])


def _outproj_body(om_ref, of_ref, x_ref, gm_ref, gf_ref, wo1_ref, wo2_ref, gffn_ref, h_o, xn2_o):
    mm = _rms_rows(om_ref[...], gm_ref[...]).astype(BF16)
    mf = _rms_rows(of_ref[...], gf_ref[...]).astype(BF16)
    h = x_ref[...] + _mm(mm, wo1_ref[...]) + _mm(mf, wo2_ref[...])
    h_o[...] = h
    xn2_o[...] = _rms_rows(h, gffn_ref[...]).astype(BF16)


def _outproj_call(om, of, x, gm, gf, wo1, wo2, gffn, tm):
    n, d = x.shape
    full = lambda a: pl.BlockSpec(a.shape, lambda i: (0,) * a.ndim)
    row = lambda w: pl.BlockSpec((tm, w), lambda i: (i, 0))
    return pl.pallas_call(
        _outproj_body,
        grid=(n // tm,),
        in_specs=[row(512), row(512), row(d)] + [full(a) for a in (gm, gf, wo1, wo2, gffn)],
        out_specs=[row(d), row(d)],
        out_shape=[jax.ShapeDtypeStruct((n, d), F32), jax.ShapeDtypeStruct((n, d), BF16)],
        compiler_params=pltpu.CompilerParams(dimension_semantics=("parallel",), vmem_limit_bytes=VMEM_LIMIT),
        name="outproj",
    )(om, of, x, gm, gf, wo1, wo2, gffn)


def _extract_max(s, iota_f):
    mx = jnp.max(s, axis=0, keepdims=True)
    idx = jnp.min(jnp.where(s == mx, iota_f, float(s.shape[0])), axis=0, keepdims=True)
    return mx, idx, iota_f == idx


def _top_rows(s, k):
    iota_f = lax.broadcasted_iota(jnp.int32, s.shape, 0).astype(F32)
    slot = lax.broadcasted_iota(jnp.int32, (k, s.shape[1]), 0)
    rank = jnp.full(s.shape, float(k), F32)
    vals = jnp.zeros((k, s.shape[1]), F32)
    for r in range(k):
        mx, _, hit = _extract_max(s, iota_f)
        rank = jnp.where(hit, float(r), rank)
        s = jnp.where(hit, -jnp.inf, s)
        vals = jnp.where(slot == r, mx, vals)
    return vals, rank


def _route_body(xn_ref, wqt_ref, keys_ref, r1_o, n0_o, e0_o, e1_o, s_s):
    k = PEER_TOPK
    tt = xn_ref.shape[0]
    qt = _nt(wqt_ref[...], xn_ref[...]).astype(BF16)
    half = PEER_DKEY // 2
    for hp in range(2 * PEER_HEADS):
        s_s[hp] = _mm(keys_ref[hp], qt[hp * half:(hp + 1) * half, :])

    def body(it, _):
        h = it // (tt // LANES)
        col = pl.ds(pl.multiple_of((it % (tt // LANES)) * LANES, LANES), LANES)
        s0 = s_s[2 * h, :, col]
        s1 = s_s[2 * h + 1, :, col]
        v0, rank0 = _top_rows(s0, k)
        v1, rank1 = _top_rows(s1, k)
        row = lax.broadcasted_iota(jnp.int32, (k, LANES), 0).astype(F32)
        taken = jnp.zeros((k, LANES), F32)
        front = v0 + v1[0:1, :]
        top = front[0:1, :]
        z = jnp.zeros((1, LANES), F32)
        for _r in range(k):
            mx, a_star, hit = _extract_max(front, row)
            taken = taken + jnp.where(hit, 1.0, 0.0)
            z = z + jnp.exp(mx - top)
            b_next = jnp.sum(jnp.where(hit, taken, 0.0), axis=0, keepdims=True)
            v1_next = jnp.sum(jnp.where(row == b_next, v1, 0.0), axis=0, keepdims=True)
            front = jnp.where(hit, jnp.where(b_next >= float(k), -jnp.inf, v0 + v1_next), front)
        n0 = jnp.zeros(rank0.shape, F32)
        for a in range(k):
            n0 = jnp.where(rank0 == float(a), taken[a:a + 1, :], n0)
        r1_o[h, :, col] = rank1.astype(r1_o.dtype)
        n0_o[h, :, col] = n0
        e0_o[h, :, col] = jnp.exp(s0 - v0[0:1, :]) / z
        e1_o[h, :, col] = jnp.exp(s1 - v1[0:1, :]).astype(e1_o.dtype)
        return 0

    lax.fori_loop(0, PEER_HEADS * (tt // LANES), body, 0)


def _route_call(xn2, wqt, keys, tt):
    n = xn2.shape[0]
    full = lambda a: pl.BlockSpec(a.shape, lambda i: (0,) * a.ndim)
    ospec = pl.BlockSpec((PEER_HEADS, PEER_KEYS, tt), lambda i: (0, 0, i))
    oshape = lambda dt: jax.ShapeDtypeStruct((PEER_HEADS, PEER_KEYS, n), dt)
    return pl.pallas_call(
        _route_body,
        grid=(n // tt,),
        in_specs=[pl.BlockSpec((tt, xn2.shape[1]), lambda i: (i, 0)), full(wqt), full(keys)],
        out_specs=[ospec] * 4,
        out_shape=[oshape(BF16), oshape(F32), oshape(F32), oshape(BF16)],
        scratch_shapes=[pltpu.VMEM((2 * PEER_HEADS, PEER_KEYS, tt), F32)],
        compiler_params=pltpu.CompilerParams(dimension_semantics=("parallel",), vmem_limit_bytes=VMEM_LIMIT),
        name="route",
    )(xn2, wqt, keys)


def _experts_body(xn_ref, h_ref, u_ref, vt_ref, r1_ref, n0_ref, e0_ref, e1_ref, y_o,
                  act_s, p_s, acc_s, nb_s, eb_s):
    e = pl.program_id(1)
    et, tt = act_s.shape
    rows_per_step = et // PEER_KEYS
    sub = nb_s.shape[2]
    jrows = 2 * sub
    njb = PEER_KEYS // jrows

    @pl.when(e == 0)
    def _():
        acc_s[...] = jnp.zeros_like(acc_s)

    act_s[...] = _nt(u_ref[...], xn_ref[...])

    first_rows = pl.ds(pl.multiple_of(e * rows_per_step, rows_per_step), rows_per_step)
    for h in range(PEER_HEADS):
        n0 = n0_ref[h, first_rows, :]
        e0 = e0_ref[h, first_rows, :]
        for il in range(rows_per_step):
            nb_s[h, il] = jnp.broadcast_to(n0[il:il + 1, :], (sub, tt)).astype(BF16)
            eb_s[h, il] = jnp.broadcast_to(e0[il:il + 1, :], (sub, tt)).astype(BF16)

    def body(it, _):
        col = pl.ds(pl.multiple_of((it // njb) * LANES, LANES), LANES)
        j0 = pl.multiple_of((it % njb) * jrows, jrows)
        g = [jnp.zeros((jrows, LANES), BF16) for _ in range(rows_per_step)]
        for h in range(PEER_HEADS):
            r1 = r1_ref[h, pl.ds(j0, jrows), col]
            e1 = e1_ref[h, pl.ds(j0, jrows), col]
            for il in range(rows_per_step):
                cnt = nb_s[h, il, :, col]
                fac = eb_s[h, il, :, col]
                cnt = jnp.concatenate([cnt, cnt], axis=0)
                fac = jnp.concatenate([fac, fac], axis=0)
                prod = e1 * fac
                g[il] = g[il] + jnp.where(r1 < cnt, prod, jnp.zeros_like(prod))
        for il in range(rows_per_step):
            rows = pl.ds(pl.multiple_of(il * PEER_KEYS + j0, jrows), jrows)
            a = act_s[rows, col]
            p_s[rows, col] = (0.5 * a * (1.0 + lax.erf(a * SQRT_HALF)) * g[il].astype(F32)).astype(BF16)
        return 0

    lax.fori_loop(0, (tt // LANES) * njb, body, 0)
    acc_s[...] += _mm(vt_ref[...], p_s[...])

    @pl.when(e == pl.num_programs(1) - 1)
    def _():
        y_o[...] = h_ref[...] + acc_s[...].T


def _experts_call(xn2, h, u, vt, r1, n0, e0, e1, tt, et):
    n, d = h.shape
    ne = u.shape[0]
    gate = pl.BlockSpec((PEER_HEADS, PEER_KEYS, tt), lambda t, e: (0, 0, t))
    return pl.pallas_call(
        _experts_body,
        grid=(n // tt, ne // et),
        in_specs=[pl.BlockSpec((tt, d), lambda t, e: (t, 0)), pl.BlockSpec((tt, d), lambda t, e: (t, 0)),
                  pl.BlockSpec((et, d), lambda t, e: (e, 0)), pl.BlockSpec((d, et), lambda t, e: (0, e)),
                  gate, gate, gate, gate],
        out_specs=pl.BlockSpec((tt, d), lambda t, e: (t, 0)),
        out_shape=jax.ShapeDtypeStruct((n, d), F32),
        scratch_shapes=[pltpu.VMEM((et, tt), F32), pltpu.VMEM((et, tt), BF16), pltpu.VMEM((d, tt), F32),
                        pltpu.VMEM((PEER_HEADS, et // PEER_KEYS, BF16_SUBLANES, tt), BF16),
                        pltpu.VMEM((PEER_HEADS, et // PEER_KEYS, BF16_SUBLANES, tt), BF16)],
        compiler_params=pltpu.CompilerParams(dimension_semantics=("parallel", "arbitrary"),
                                             vmem_limit_bytes=VMEM_LIMIT),
        name="experts",
    )(xn2, h, u, vt, r1, n0, e0, e1)


def _proj_columns():
    o_qn, o_qpe, o_ckv, o_kpe = 0, 512, 768, 896
    o_fq, o_fk, o_fv, o_fl = 928, 1440, 1952, 2464
    swap = lambda l: (l + MLA_ROPE // 2) % MLA_ROPE
    cols = list(range(o_qn, o_qn + 512))
    cols += list(range(o_qpe, o_qpe + 256))
    cols += [o_qpe + MLA_ROPE * h + swap(l) for h in range(MLA_HEADS) for l in range(MLA_ROPE)]
    cols += list(range(o_ckv, o_ckv + KV_RANK))
    cols += [o_kpe + l for _ in range(4) for l in range(MLA_ROPE)]
    cols += [o_kpe + swap(l) for _ in range(4) for l in range(MLA_ROPE)]
    cols += list(range(o_fq, o_fq + 512)) + list(range(o_fk, o_fk + 512)) + list(range(o_fv, o_fv + 512))
    cols += list(range(o_fl, o_fl + FOX_HEADS)) + [-1] * (LANES - FOX_HEADS)
    assert len(cols) == C_END
    return np.asarray(cols, np.int32)


def _block_diag_mean(size, group):
    idx = np.arange(size) // group
    return jnp.asarray((idx[:, None] == idx[None, :]).astype(np.float32) / group, BF16)


def _rope_tables(pos):
    half = MLA_ROPE // 2
    inv = jnp.power(ROPE_THETA, -jnp.arange(half, dtype=F32) / half)
    ang = pos.astype(F32)[:, None] * inv
    cos = jnp.concatenate([jnp.cos(ang), jnp.cos(ang)], axis=1)
    sin = jnp.concatenate([-jnp.sin(ang), jnp.sin(ang)], axis=1)
    return (jnp.tile(cos, (1, MLA_HEADS)), jnp.tile(sin, (1, MLA_HEADS)), jnp.tile(cos, (1, 4)), jnp.tile(sin, (1, 4)))


def _tile_for(n, cap):
    t = min(n, cap)
    while n % t:
        t //= 2
    return t


def kernel(x_prompt, x_sample, cache_mla_latent, cache_mla_kpe, cache_fox_k, cache_fox_v, cache_fox_logf, page_table, g_attn, w_in, b_forget, g_q_nope, g_q_pe, g_ckv, g_k_pe, w_ukv, g_k_nope, g_fox_q, g_fox_k, g_out_mla, g_out_fox, w_o, g_ffn, w_peer_q, peer_sub_keys, peer_u, peer_v):
    depth = g_attn.shape[0]
    assert depth == 1, "single-layer trunk"
    b, t, d = x_prompt.shape
    db, dt, _ = x_sample.shape
    assert dt == 1
    n_pages = page_table.shape[1]
    page = cache_mla_latent.shape[2]
    past = n_pages * page
    l = 0

    cols = _proj_columns()
    w_pad = jnp.concatenate([w_in[l], jnp.zeros((d, 1), F32)], axis=1)
    w_arr = w_pad[:, jnp.where(cols < 0, w_in.shape[2], cols)].astype(BF16)
    ukv = w_ukv[l].reshape(KV_RANK, MLA_HEADS, MLA_NOPE + MLA_V)
    wk = ukv[:, :, :MLA_NOPE].reshape(KV_RANK, MLA_HEADS * MLA_NOPE).astype(BF16)
    wv = ukv[:, :, MLA_NOPE:].reshape(KV_RANK, MLA_HEADS * MLA_V).astype(BF16)
    wukv = jnp.concatenate([wk, wv], axis=1)
    swap = (np.arange(MLA_ROPE) + MLA_ROPE // 2) % MLA_ROPE
    s_mla = (MLA_NOPE + MLA_ROPE) ** -0.5
    s_fox = FOX_DIM ** -0.5
    row = lambda v: v.reshape(1, -1).astype(F32)
    consts = [
        row(g_attn[l]), w_arr, wukv,
        row(jnp.tile(g_q_nope[l], MLA_HEADS) * s_mla), row(jnp.tile(g_q_pe[l], MLA_HEADS) * s_mla),
        row(jnp.tile(g_q_pe[l][swap], MLA_HEADS) * s_mla), row(g_ckv[l]),
        row(jnp.tile(g_k_pe[l], 4)), row(jnp.tile(g_k_pe[l][swap], 4)),
        row(jnp.tile(g_fox_q[l], FOX_HEADS) * s_fox), row(jnp.tile(g_fox_k[l], FOX_HEADS)),
        row(jnp.tile(g_k_nope[l], MLA_HEADS)),
        row(jnp.concatenate([b_forget[l], jnp.zeros((LANES - FOX_HEADS,), F32)])),
        _block_diag_mean(256, 64), _block_diag_mean(256, 32), _block_diag_mean(128, 32),
    ]

    n = b * t
    tm = _tile_for(t, 256)
    tabs_p = _rope_tables(jnp.arange(t, dtype=jnp.int32))
    xp = x_prompt.reshape(n, d)
    (qn, qpe, kn, kpe4, vm, fq, fk, fv, ckv_p, kpe_p, fk_p, fv_p, logf_p) = _proj_call(xp, tm, t // tm, consts, tabs_p)
    blk = _tile_for(t, 256)
    tri = jnp.asarray(np.triu(np.ones((blk, blk), np.float32)), BF16)
    c_t = _cumsum_call(logf_p.reshape(b, t, FOX_HEADS).transpose(0, 2, 1), tri)
    ccol = c_t.reshape(b, 4, 2, t).transpose(1, 0, 3, 2).reshape(4, n, 2)
    crow = c_t.reshape(b * 4, 2, t)
    tq = _tile_for(t, 256)
    om_p, of_p = _attn_call(qn, qpe, kn, kpe4, vm, fq, fk, fv, ccol, crow, b, t, tq)

    tabs_s = _rope_tables(jnp.full((db,), past, jnp.int32))
    xs = x_sample.reshape(db, d)
    (qn_s, qpe_s, _, _, _, fq_s, _, _, ckv_s, kpe_s, fk_s, fv_s, logf_s) = _proj_call(xs, db, 1, consts, tabs_s)
    head_of = np.arange(512) // MLA_NOPE
    blockmask = jnp.asarray(head_of[None, :] == np.arange(MLA_HEADS)[:, None])
    qnblk = jnp.where(blockmask[None], qn_s[:, None, :], jnp.zeros((), BF16))
    qpeblk = qpe_s.reshape(db, MLA_HEADS, MLA_ROPE)
    n_slots = 8 if n_pages % 8 == 0 else 1
    fdim = FOX_HEADS * FOX_DIM
    om_s, of_s = _decode_call(
        page_table, qnblk, qpeblk, fq_s.astype(F32).reshape(db, fdim, 1),
        ckv_s.reshape(db, 1, KV_RANK), kpe_s.reshape(db, MLA_ROPE, 1),
        fk_s.reshape(db, fdim, 1), fv_s.reshape(db, fdim, 1), logf_s.reshape(db, FOX_HEADS, 1),
        wk.T, wv, consts[11],
        cache_mla_latent[l], cache_mla_kpe[l].transpose(0, 2, 1),
        cache_fox_k[l].transpose(0, 2, 3, 1).reshape(-1, fdim, page),
        cache_fox_v[l].transpose(0, 2, 3, 1).reshape(-1, fdim, page),
        cache_fox_logf[l].transpose(0, 2, 1), n_slots)

    wo1 = w_o[l][:512].astype(BF16)
    wo2 = w_o[l][512:].astype(BF16)
    gm, gf, gffn = row(g_out_mla[l]), row(g_out_fox[l]), row(g_ffn[l])
    wqt = w_peer_q[l].T.astype(BF16)
    keys = peer_sub_keys[l].reshape(2 * PEER_HEADS, PEER_KEYS, PEER_DKEY // 2).astype(BF16)
    u = peer_u[l].astype(BF16)
    vt = peer_v[l].T.astype(BF16)

    def tail(om, of, x, tm_o, tt, et):
        h, xn2 = _outproj_call(om, of, x, gm, gf, wo1, wo2, gffn, tm_o)
        pad = (-h.shape[0]) % tt
        if pad:
            h = jnp.pad(h, ((0, pad), (0, 0)))
            xn2 = jnp.pad(xn2, ((0, pad), (0, 0)))
        gates = _route_call(xn2, wqt, keys, tt)
        y = _experts_call(xn2, h, u, vt, *gates, tt, et)
        return y[:x.shape[0]]

    y_p = tail(om_p, of_p, xp, tm, _tile_for(n, 512), 1024)
    y_s = tail(om_s.reshape(db, 512), of_s.reshape(db, 512), xs, db, LANES, 1024)

    shp = lambda a, *s: a.reshape((depth,) + s)
    return (y_p.reshape(b, t, d), y_s.reshape(db, 1, d),
            shp(ckv_p, b, t, KV_RANK), shp(kpe_p, b, t, MLA_ROPE),
            shp(fk_p, b, t, FOX_HEADS, FOX_DIM), shp(fv_p, b, t, FOX_HEADS, FOX_DIM), shp(logf_p, b, t, FOX_HEADS),
            shp(ckv_s, db, 1, KV_RANK), shp(kpe_s, db, 1, MLA_ROPE),
            shp(fk_s, db, 1, FOX_HEADS, FOX_DIM), shp(fv_s, db, 1, FOX_HEADS, FOX_DIM), shp(logf_s, db, 1, FOX_HEADS))
```

```python
import functools
import math

import numpy as np
import jax
import jax.numpy as jnp
from jax import lax
from jax.experimental import pallas as pl
from jax.experimental.pallas import tpu as pltpu

F32 = jnp.float32
BF16 = jnp.bfloat16

MLA_HEADS, MLA_NOPE, MLA_ROPE, MLA_V, KV_RANK = 8, 64, 32, 64, 128
ROPE_THETA = 10000.0
FOX_HEADS, FOX_DIM = 8, 64
PEER_HEADS, PEER_KEYS, PEER_DKEY, PEER_TOPK = 8, 128, 256, 16
NORM_EPS = 1e-6
NEG_INF = -1e30
SQRT_HALF = 0.7071067811865476

LANES = 128
BF16_SUBLANES = 16
VMEM_LIMIT = 56 * 1024 * 1024

C_QN, C_QPE, C_QPES, C_CKV, C_KPE, C_KPES, C_FQ, C_FK, C_FV, C_FL, C_END = (
    0, 512, 768, 1024, 1152, 1280, 1408, 1920, 2432, 2944, 3072)


def _nt(a, b):
    return lax.dot_general(a, b, (((1,), (1,)), ((), ())), preferred_element_type=F32)


def _mm(a, b):
    return jnp.dot(a, b, preferred_element_type=F32)


def _rms_rows(x, g):
    return x * lax.rsqrt(jnp.mean(x * x, axis=-1, keepdims=True) + NORM_EPS) * g


def _split2(x):
    hi = x.astype(BF16)
    lo = (x - hi.astype(F32)).astype(BF16)
    return hi, lo


def _group_rsqrt(y, m_ref):
    c = m_ref.shape[0]
    outs = []
    for a in range(0, y.shape[1], c):
        ya = y[:, a:a + c]
        hi, lo = _split2(ya * ya)
        ms = _mm(hi, m_ref[...]) + _mm(lo, m_ref[...])
        outs.append(lax.rsqrt(ms + NORM_EPS))
    return outs[0] if len(outs) == 1 else jnp.concatenate(outs, axis=1)


def _proj_body(x_ref, ga_ref, w_ref, wukv_ref, cq_ref, sq_ref, ck_ref, sk_ref,
               gqn_ref, gqp_ref, gqps_ref, gckv_ref, gkp_ref, gkps_ref, gfq_ref, gfk_ref, gkn_ref, bf_ref,
               m64_ref, m32_ref, m32s_ref,
               qn_o, qpe_o, kn_o, kpe4_o, vm_o, fq_o, fk_o, fv_o, ckv_o, kpe_o, fk32_o, fv32_o, logf_o):
    xn = _rms_rows(x_ref[...], ga_ref[...]).astype(BF16)

    def proj(a, b):
        return _mm(xn, w_ref[:, a:b])

    y = proj(C_QN, C_QPE)
    qn_o[...] = (y * _group_rsqrt(y, m64_ref) * gqn_ref[...]).astype(BF16)

    a = proj(C_QPE, C_QPES)
    b = proj(C_QPES, C_CKV)
    r = _group_rsqrt(a, m32_ref)
    qpe_o[...] = (r * (a * gqp_ref[...] * cq_ref[...] + b * gqps_ref[...] * sq_ref[...])).astype(BF16)

    ckv = _rms_rows(proj(C_CKV, C_KPE), gckv_ref[...])
    ckv_o[...] = ckv

    a = proj(C_KPE, C_KPES)
    b = proj(C_KPES, C_FQ)
    r = _group_rsqrt(a, m32s_ref)
    kpe4 = r * (a * gkp_ref[...] * ck_ref[...] + b * gkps_ref[...] * sk_ref[...])
    kpe4_o[...] = kpe4.astype(BF16)
    kpe_o[...] = kpe4[:, :MLA_ROPE]

    y = proj(C_FQ, C_FK)
    fq_o[...] = (y * _group_rsqrt(y, m64_ref) * gfq_ref[...]).astype(BF16)

    y = proj(C_FK, C_FV)
    fk = y * _group_rsqrt(y, m64_ref) * gfk_ref[...]
    fk32_o[...] = fk
    fk_o[...] = fk.astype(BF16)

    y = proj(C_FV, C_FL)
    fv32_o[...] = y
    fv_o[...] = y.astype(BF16)

    z = proj(C_FL, C_END) + bf_ref[...]
    logf = jnp.minimum(z, 0.0) - jnp.log1p(jnp.exp(-jnp.abs(z)))
    logf_o[...] = logf[:, :FOX_HEADS]

    kv = _mm(ckv.astype(BF16), wukv_ref[...])
    kn = kv[:, :MLA_HEADS * MLA_NOPE]
    kn_o[...] = (kn * _group_rsqrt(kn, m64_ref) * gkn_ref[...]).astype(BF16)
    vm_o[...] = kv[:, MLA_HEADS * MLA_NOPE:].astype(BF16)


def _proj_call(x, tm, n_pos_blocks, consts, tabs):
    n, d = x.shape
    cq, sq, ck, sk = tabs
    full = lambda a: pl.BlockSpec(a.shape, lambda i: (0,) * a.ndim)
    row = lambda w: pl.BlockSpec((tm, w), lambda i: (i, 0))
    tab = lambda w: pl.BlockSpec((tm, w), lambda i: (i % n_pos_blocks, 0))
    in_specs = ([row(d), full(consts[0]), full(consts[1]), full(consts[2]), tab(256), tab(256), tab(128), tab(128)]
                + [full(c) for c in consts[3:]])
    widths = [(512, BF16), (256, BF16), (512, BF16), (128, BF16), (512, BF16), (512, BF16), (512, BF16), (512, BF16),
              (KV_RANK, F32), (MLA_ROPE, F32), (512, F32), (512, F32), (FOX_HEADS, F32)]
    return pl.pallas_call(
        _proj_body,
        grid=(n // tm,),
        in_specs=in_specs,
        out_specs=[row(w) for w, _ in widths],
        out_shape=[jax.ShapeDtypeStruct((n, w), dt) for w, dt in widths],
        compiler_params=pltpu.CompilerParams(dimension_semantics=("parallel",), vmem_limit_bytes=VMEM_LIMIT),
        name="proj",
    )(x, consts[0], consts[1], consts[2], cq, sq, ck, sk, *consts[3:])


def _cumsum_body(x_ref, tri_ref, o_ref):
    t = x_ref.shape[1]
    blk = tri_ref.shape[0]
    carry = jnp.zeros((x_ref.shape[0], 1), F32)
    for a in range(0, t, blk):
        x = x_ref[:, a:a + blk]
        hi, lo = _split2(x)
        lo2 = (x - hi.astype(F32) - lo.astype(F32)).astype(BF16)
        c = _mm(hi, tri_ref[...]) + _mm(lo, tri_ref[...]) + _mm(lo2, tri_ref[...]) + carry
        o_ref[:, a:a + blk] = c
        carry = c[:, blk - 1:blk]


def _cumsum_call(logf_t, tri):
    b, h, t = logf_t.shape
    return pl.pallas_call(
        _cumsum_body,
        grid=(b,),
        in_specs=[pl.BlockSpec((None, h, t), lambda i: (i, 0, 0)), pl.BlockSpec(tri.shape, lambda i: (0, 0))],
        out_specs=pl.BlockSpec((None, h, t), lambda i: (i, 0, 0)),
        out_shape=jax.ShapeDtypeStruct((b, h, t), F32),
        compiler_params=pltpu.CompilerParams(dimension_semantics=("parallel",)),
        name="cumsum",
    )(logf_t, tri)


def _softmax_step(s, m, l, acc, v):
    m_new = jnp.maximum(m, jnp.max(s, axis=-1, keepdims=True))
    alpha = jnp.exp(m - m_new)
    p = jnp.exp(s - m_new)
    l = l * alpha + jnp.sum(p, axis=-1, keepdims=True)
    acc = acc * alpha + _mm(p.astype(BF16), v)
    return m_new, l, acc


def _attn_body(qn_ref, qpe_ref, kn_ref, kpe4_ref, vm_ref, fq_ref, fk_ref, fv_ref, ccol_ref, crow_ref,
               om_o, of_o, *, tq):
    hp = pl.program_id(1)
    qi = pl.program_id(2)
    lane_q = lax.broadcasted_iota(jnp.int32, (tq, 2 * LANES), 1)
    lane_o = lax.broadcasted_iota(jnp.int32, (tq, LANES), 1)
    q_m = jnp.concatenate([qn_ref[...], qpe_ref[...]], axis=1).astype(F32)
    q_f = fq_ref[...].astype(F32)
    qm, qf, cq = [], [], []
    for hh in range(2):
        quad = 2 * (hp % 2) + hh
        want = jnp.where(lane_q < LANES, hh * MLA_NOPE, LANES + quad * MLA_ROPE)
        start = jnp.where(lane_q < LANES, (lane_q // MLA_NOPE) * MLA_NOPE, (lane_q // MLA_ROPE) * MLA_ROPE)
        qm.append(jnp.where(start == want, q_m, 0.0).astype(BF16))
        qf.append(jnp.where(lane_o // FOX_DIM == hh, q_f, 0.0).astype(BF16))
        cq.append(ccol_ref[:, hh:hh + 1])

    row_id = lax.broadcasted_iota(jnp.int32, (tq, tq), 0)
    col_id = lax.broadcasted_iota(jnp.int32, (tq, tq), 1)
    causal = col_id <= row_id

    def block(j, carry, masked):
        off = pl.multiple_of(j * tq, tq)
        k_m = jnp.concatenate([kn_ref[pl.ds(off, tq), :], kpe4_ref[pl.ds(off, tq), :]], axis=1)
        v_m = vm_ref[pl.ds(off, tq), :]
        k_f = fk_ref[pl.ds(off, tq), :]
        v_f = fv_ref[pl.ds(off, tq), :]
        out = []
        for hh in range(2):
            s = _nt(qm[hh], k_m)
            if masked:
                s = jnp.where(causal, s, NEG_INF)
            out.append(_softmax_step(s, *carry[hh], v_m))
        for hh in range(2):
            s = _nt(qf[hh], k_f) + cq[hh] - crow_ref[hh:hh + 1, pl.ds(off, tq)]
            if masked:
                s = jnp.where(causal, s, NEG_INF)
            out.append(_softmax_step(s, *carry[2 + hh], v_f))
        return tuple(out)

    init = tuple((jnp.full((tq, 1), NEG_INF, F32), jnp.zeros((tq, 1), F32), jnp.zeros((tq, LANES), F32))
                 for _ in range(4))
    carry = lax.fori_loop(0, qi, lambda j, c: block(j, c, False), init)
    carry = block(qi, carry, True)
    first = lane_o < MLA_V
    om_o[...] = jnp.where(first, carry[0][2] / carry[0][1], carry[1][2] / carry[1][1])
    of_o[...] = jnp.where(first, carry[2][2] / carry[2][1], carry[3][2] / carry[3][1])


def _attn_call(qn, qpe, kn, kpe4, vm, fq, fk, fv, ccol, crow, b, t, tq):
    n = b * t
    nq = t // tq
    qspec = lambda cmap: pl.BlockSpec((tq, LANES), lambda bi, hp, qi: (bi * nq + qi, cmap(hp)))
    kspec = lambda cmap: pl.BlockSpec((t, LANES), lambda bi, hp, qi: (bi, cmap(hp)))
    pair = lambda hp: hp
    in_specs = [qspec(pair), qspec(lambda hp: hp // 2), kspec(pair), kspec(lambda hp: 0), kspec(pair),
                qspec(pair), kspec(pair), kspec(pair),
                pl.BlockSpec((None, tq, 2), lambda bi, hp, qi: (hp, bi * nq + qi, 0)),
                pl.BlockSpec((None, 2, t), lambda bi, hp, qi: (bi * 4 + hp, 0, 0))]
    ospec = pl.BlockSpec((tq, LANES), lambda bi, hp, qi: (bi * nq + qi, hp))
    return pl.pallas_call(
        functools.partial(_attn_body, tq=tq),
        grid=(b, 4, nq),
        in_specs=in_specs,
        out_specs=[ospec, ospec],
        out_shape=[jax.ShapeDtypeStruct((n, 512), F32)] * 2,
        compiler_params=pltpu.CompilerParams(dimension_semantics=("parallel", "parallel", "arbitrary"),
                                             vmem_limit_bytes=VMEM_LIMIT),
        name="attn",
    )(qn, qpe, kn, kpe4, vm, fq, fk, fv, ccol, crow)


def _suffix_incl(x):
    n = x.shape[1]
    lane = lax.broadcasted_iota(jnp.int32, (1, n), 1)
    d = 1
    while d < n:
        x = x + jnp.where(lane + d < n, pltpu.roll(x, n - d, 1), 0.0)
        d *= 2
    return x


def _decode_body(pt_ref, qn_ref, qpe_ref, fqc_ref, latn_ref, kpen_ref, fkn_ref, fvn_ref, logfn_ref,
                 wkt_ref, wv_ref, gkn_ref, *rest, n_slots):
    del pt_ref
    pages = rest[:5 * n_slots]
    om_o, of_o = rest[5 * n_slots:5 * n_slots + 2]
    wkq_s, qfx_s, mm_s, lm_s, am_s, mf_s, lf_s, af_s, cr_s = rest[5 * n_slots + 2:]
    step = pl.program_id(1)
    qpe = qpe_ref[0]
    nh = MLA_HEADS
    kdim = MLA_HEADS * MLA_NOPE
    fdim = FOX_HEADS * FOX_DIM

    def head_sums(x):
        return x.reshape(x.shape[0] // FOX_DIM, FOX_DIM, x.shape[1]).sum(axis=1)

    def mla_scores(lat, kpe_t):
        latb = lat.astype(BF16)
        kq = _nt(wkq_s[...], latb)
        kraw_t = kq[:kdim]
        ssq = head_sums(kraw_t * kraw_t) * (1.0 / MLA_NOPE)
        s = kq[kdim:kdim + nh] * lax.rsqrt(ssq + NORM_EPS) + _mm(qpe, kpe_t.astype(BF16))
        return s, latb

    def fox_scores(fk_t):
        return head_sums(fk_t * qfx_s[...])

    @pl.when(step == 0)
    def _():
        qt = (qn_ref[0].astype(F32) * gkn_ref[...]).astype(BF16)
        qabs = _mm(qt, wkt_ref[...])
        wkq_s[0:kdim, :] = wkt_ref[...]
        wkq_s[kdim:kdim + 2 * nh, :] = jnp.concatenate([qabs, jnp.zeros_like(qabs)], axis=0).astype(BF16)
        qfx_s[...] = jnp.broadcast_to(fqc_ref[0], qfx_s.shape)
        s, latb = mla_scores(jnp.broadcast_to(latn_ref[0], (LANES, KV_RANK)),
                             jnp.broadcast_to(kpen_ref[0], (MLA_ROPE, LANES)))
        mm_s[...] = s[:, 0:1]
        lm_s[...] = jnp.ones_like(lm_s)
        am_s[...] = latb[0:nh].astype(F32)
        mf_s[...] = fox_scores(jnp.broadcast_to(fkn_ref[0], (fdim, LANES)))[:, 0:1]
        lf_s[...] = jnp.ones_like(lf_s)
        lane = lax.broadcasted_iota(jnp.int32, af_s.shape, 1)
        af_s[...] = jnp.where(lane == 0, jnp.broadcast_to(fvn_ref[0], af_s.shape), 0.0)
        cr_s[...] = logfn_ref[0]

    def softmax_update(s, m_s, l_s):
        m_old = m_s[...]
        m_new = jnp.maximum(m_old, jnp.max(s, axis=-1, keepdims=True))
        alpha = jnp.exp(m_old - m_new)
        p = jnp.exp(s - m_new)
        l_s[...] = l_s[...] * alpha + jnp.sum(p, axis=-1, keepdims=True)
        m_s[...] = m_new
        return p, alpha

    scores, latbs = zip(*[mla_scores(pages[5 * g][0], pages[5 * g + 1][0]) for g in range(n_slots)])
    p, alpha = softmax_update(jnp.concatenate(scores, axis=1), mm_s, lm_s)
    pv = _mm(p[:, 0:LANES].astype(BF16), latbs[0])
    for g in range(1, n_slots):
        pv = pv + _mm(p[:, g * LANES:(g + 1) * LANES].astype(BF16), latbs[g])
    am_s[...] = am_s[...] * alpha + pv

    carry = cr_s[...]
    scores = []
    for g in range(n_slots):
        x = pages[5 * g + 4][0]
        incl = _suffix_incl(x)
        scores.append(fox_scores(pages[5 * g + 2][0]) + (incl - x + carry))
        carry = carry + incl[:, 0:1]
    cr_s[...] = carry
    p, alpha = softmax_update(jnp.concatenate(scores, axis=1), mf_s, lf_s)
    for h in range(FOX_HEADS):
        rows = slice(h * FOX_DIM, (h + 1) * FOX_DIM)
        acc = af_s[rows, :] * alpha[h:h + 1, :]
        for g in range(n_slots):
            acc = acc + pages[5 * g + 3][0, rows, :] * p[h:h + 1, g * LANES:(g + 1) * LANES]
        af_s[rows, :] = acc

    @pl.when(step == pl.num_programs(1) - 1)
    def _():
        hi, lo = _split2(am_s[...] / lm_s[...])
        ov = _mm(hi, wv_ref[...]) + _mm(lo, wv_ref[...])
        rows, width = ov.shape
        diag = (lax.broadcasted_iota(jnp.int32, (rows, width), 1) // MLA_V
                == lax.broadcasted_iota(jnp.int32, (rows, width), 0))
        om_o[0] = jnp.sum(jnp.where(diag, ov, 0.0), axis=0, keepdims=True)
        linv = 1.0 / lf_s[...]
        scale = jnp.concatenate([jnp.broadcast_to(linv[h:h + 1, :], (FOX_DIM, 1)) for h in range(FOX_HEADS)], axis=0)
        of_o[0] = jnp.sum(af_s[...], axis=1, keepdims=True) * scale


def _decode_call(page_table, qnblk, qpeblk, fqc, latn, kpen, fkn, fvn, logfn, wkt, wv, gkn,
                 c_lat, c_kpe_t, c_fk_t, c_fv_t, c_lf_t, n_slots):
    db, n_pages = page_table.shape
    steps = n_pages // n_slots
    seq = lambda a: pl.BlockSpec((1,) + a.shape[1:], lambda b, s, pt: (b,) + (0,) * (a.ndim - 1))
    full = lambda a: pl.BlockSpec(a.shape, lambda b, s, pt: (0,) * a.ndim)

    def page(a, g):
        return pl.BlockSpec((1,) + a.shape[1:],
                            lambda b, s, pt: (pt[b, n_pages - 1 - (s * n_slots + g)],) + (0,) * (a.ndim - 1))

    caches = (c_lat, c_kpe_t, c_fk_t, c_fv_t, c_lf_t)
    in_specs = ([seq(a) for a in (qnblk, qpeblk, fqc, latn, kpen, fkn, fvn, logfn)]
                + [full(a) for a in (wkt, wv, gkn)]
                + [page(a, g) for g in range(n_slots) for a in caches])
    h = MLA_HEADS
    kdim = MLA_HEADS * MLA_NOPE
    fdim = FOX_HEADS * FOX_DIM
    col = lambda: pltpu.VMEM((h, 1), F32)
    scratch = [pltpu.VMEM((kdim + 2 * h, KV_RANK), BF16), pltpu.VMEM((fdim, LANES), F32),
               col(), col(), pltpu.VMEM((h, KV_RANK), F32), col(), col(), pltpu.VMEM((fdim, LANES), F32), col()]
    return pl.pallas_call(
        functools.partial(_decode_body, n_slots=n_slots),
        grid_spec=pltpu.PrefetchScalarGridSpec(
            num_scalar_prefetch=1, grid=(db, steps), in_specs=in_specs,
            out_specs=[pl.BlockSpec((1, 1, kdim), lambda b, s, pt: (b, 0, 0)),
                       pl.BlockSpec((1, fdim, 1), lambda b, s, pt: (b, 0, 0))],
            scratch_shapes=scratch),
        out_shape=[jax.ShapeDtypeStruct((db, 1, kdim), F32), jax.ShapeDtypeStruct((db, fdim, 1), F32)],
        compiler_params=pltpu.CompilerParams(dimension_semantics=("parallel", "arbitrary"),
                                             vmem_limit_bytes=VMEM_LIMIT),
        name="decode",
    )(page_table, qnblk, qpeblk, fqc, latn, kpen, fkn, fvn, logfn, wkt, wv, gkn,
      *[a for _ in range(n_slots) for a in caches])


def _outproj_body(om_ref, of_ref, x_ref, gm_ref, gf_ref, wo1_ref, wo2_ref, gffn_ref, h_o, xn2_o):
    mm = _rms_rows(om_ref[...], gm_ref[...]).astype(BF16)
    mf = _rms_rows(of_ref[...], gf_ref[...]).astype(BF16)
    h = x_ref[...] + _mm(mm, wo1_ref[...]) + _mm(mf, wo2_ref[...])
    h_o[...] = h
    xn2_o[...] = _rms_rows(h, gffn_ref[...]).astype(BF16)


def _outproj_call(om, of, x, gm, gf, wo1, wo2, gffn, tm):
    n, d = x.shape
    full = lambda a: pl.BlockSpec(a.shape, lambda i: (0,) * a.ndim)
    row = lambda w: pl.BlockSpec((tm, w), lambda i: (i, 0))
    return pl.pallas_call(
        _outproj_body,
        grid=(n // tm,),
        in_specs=[row(512), row(512), row(d)] + [full(a) for a in (gm, gf, wo1, wo2, gffn)],
        out_specs=[row(d), row(d)],
        out_shape=[jax.ShapeDtypeStruct((n, d), F32), jax.ShapeDtypeStruct((n, d), BF16)],
        compiler_params=pltpu.CompilerParams(dimension_semantics=("parallel",), vmem_limit_bytes=VMEM_LIMIT),
        name="outproj",
    )(om, of, x, gm, gf, wo1, wo2, gffn)


def _extract_max(s, iota_f):
    mx = jnp.max(s, axis=0, keepdims=True)
    idx = jnp.min(jnp.where(s == mx, iota_f, float(s.shape[0])), axis=0, keepdims=True)
    return mx, idx, iota_f == idx


def _top_rows(s, k):
    iota_f = lax.broadcasted_iota(jnp.int32, s.shape, 0).astype(F32)
    slot = lax.broadcasted_iota(jnp.int32, (k, s.shape[1]), 0)
    rank = jnp.full(s.shape, float(k), F32)
    vals = jnp.zeros((k, s.shape[1]), F32)
    for r in range(k):
        mx, _, hit = _extract_max(s, iota_f)
        rank = jnp.where(hit, float(r), rank)
        s = jnp.where(hit, -jnp.inf, s)
        vals = jnp.where(slot == r, mx, vals)
    return vals, rank


def _route_body(xn_ref, wqt_ref, keys_ref, r1_o, n0_o, e0_o, e1_o, s_s):
    k = PEER_TOPK
    tt = xn_ref.shape[0]
    qt = _nt(wqt_ref[...], xn_ref[...]).astype(BF16)
    half = PEER_DKEY // 2
    for hp in range(2 * PEER_HEADS):
        s_s[hp] = _mm(keys_ref[hp], qt[hp * half:(hp + 1) * half, :])

    def body(it, _):
        h = it // (tt // LANES)
        col = pl.ds(pl.multiple_of((it % (tt // LANES)) * LANES, LANES), LANES)
        s0 = s_s[2 * h, :, col]
        s1 = s_s[2 * h + 1, :, col]
        v0, rank0 = _top_rows(s0, k)
        v1, rank1 = _top_rows(s1, k)
        row = lax.broadcasted_iota(jnp.int32, (k, LANES), 0).astype(F32)
        taken = jnp.zeros((k, LANES), F32)
        front = v0 + v1[0:1, :]
        top = front[0:1, :]
        z = jnp.zeros((1, LANES), F32)
        for _r in range(k):
            mx, a_star, hit = _extract_max(front, row)
            taken = taken + jnp.where(hit, 1.0, 0.0)
            z = z + jnp.exp(mx - top)
            b_next = jnp.sum(jnp.where(hit, taken, 0.0), axis=0, keepdims=True)
            v1_next = jnp.sum(jnp.where(row == b_next, v1, 0.0), axis=0, keepdims=True)
            front = jnp.where(hit, jnp.where(b_next >= float(k), -jnp.inf, v0 + v1_next), front)
        n0 = jnp.zeros(rank0.shape, F32)
        for a in range(k):
            n0 = jnp.where(rank0 == float(a), taken[a:a + 1, :], n0)
        r1_o[h, :, col] = rank1.astype(r1_o.dtype)
        n0_o[h, :, col] = n0
        e0_o[h, :, col] = jnp.exp(s0 - v0[0:1, :]) / z
        e1_o[h, :, col] = jnp.exp(s1 - v1[0:1, :]).astype(e1_o.dtype)
        return 0

    lax.fori_loop(0, PEER_HEADS * (tt // LANES), body, 0)


def _route_call(xn2, wqt, keys, tt):
    n = xn2.shape[0]
    full = lambda a: pl.BlockSpec(a.shape, lambda i: (0,) * a.ndim)
    ospec = pl.BlockSpec((PEER_HEADS, PEER_KEYS, tt), lambda i: (0, 0, i))
    oshape = lambda dt: jax.ShapeDtypeStruct((PEER_HEADS, PEER_KEYS, n), dt)
    return pl.pallas_call(
        _route_body,
        grid=(n // tt,),
        in_specs=[pl.BlockSpec((tt, xn2.shape[1]), lambda i: (i, 0)), full(wqt), full(keys)],
        out_specs=[ospec] * 4,
        out_shape=[oshape(F32), oshape(F32), oshape(F32), oshape(F32)],
        scratch_shapes=[pltpu.VMEM((2 * PEER_HEADS, PEER_KEYS, tt), F32)],
        compiler_params=pltpu.CompilerParams(dimension_semantics=("parallel",), vmem_limit_bytes=VMEM_LIMIT),
        name="route",
    )(xn2, wqt, keys)


def _experts_body(xn_ref, h_ref, u_ref, vt_ref, r1_ref, n0_ref, e0_ref, e1_ref, y_o,
                  act_s, p_s, acc_s, cnt_s, fac_s):
    e = pl.program_id(1)
    et, tt = act_s.shape
    rows_per_step = et // PEER_KEYS
    nch = tt // LANES
    jrows = 32
    njb = PEER_KEYS // jrows

    @pl.when(e == 0)
    def _():
        acc_s[...] = jnp.zeros_like(acc_s)

    act_s[...] = _nt(u_ref[...], xn_ref[...])

    first_rows = pl.ds(pl.multiple_of(e * rows_per_step, rows_per_step), rows_per_step)
    for h in range(PEER_HEADS):
        for c in range(nch):
            cnt_s[h, c] = n0_ref[h, first_rows, c * LANES:(c + 1) * LANES]
            fac_s[h, c] = e0_ref[h, first_rows, c * LANES:(c + 1) * LANES]

    def body(it, _):
        c = it // njb
        col = pl.ds(pl.multiple_of(c * LANES, LANES), LANES)
        j0 = pl.multiple_of((it % njb) * jrows, jrows)
        g = [jnp.zeros((jrows, LANES), F32) for _ in range(rows_per_step)]
        for h in range(PEER_HEADS):
            r1 = r1_ref[h, pl.ds(j0, jrows), col]
            e1 = e1_ref[h, pl.ds(j0, jrows), col]
            for il in range(rows_per_step):
                cnt = cnt_s[h, c, il:il + 1, :]
                fac = fac_s[h, c, il:il + 1, :]
                g[il] = g[il] + jnp.where(r1 < cnt, e1 * fac, 0.0)
        for il in range(rows_per_step):
            rows = pl.ds(pl.multiple_of(il * PEER_KEYS + j0, jrows), jrows)
            a = act_s[rows, col]
            p_s[rows, col] = (0.5 * a * (1.0 + lax.erf(a * SQRT_HALF)) * g[il]).astype(BF16)
        return 0

    lax.fori_loop(0, nch * njb, body, 0)
    acc_s[...] += _mm(vt_ref[...], p_s[...])

    @pl.when(e == pl.num_programs(1) - 1)
    def _():
        y_o[...] = h_ref[...] + acc_s[...].T


def _experts_call(xn2, h, u, vt, r1, n0, e0, e1, tt, et):
    n, d = h.shape
    ne = u.shape[0]
    gate = pl.BlockSpec((PEER_HEADS, PEER_KEYS, tt), lambda t, e: (0, 0, t))
    return pl.pallas_call(
        _experts_body,
        grid=(n // tt, ne // et),
        in_specs=[pl.BlockSpec((tt, d), lambda t, e: (t, 0)), pl.BlockSpec((tt, d), lambda t, e: (t, 0)),
                  pl.BlockSpec((et, d), lambda t, e: (e, 0)), pl.BlockSpec((d, et), lambda t, e: (0, e)),
                  gate, gate, gate, gate],
        out_specs=pl.BlockSpec((tt, d), lambda t, e: (t, 0)),
        out_shape=jax.ShapeDtypeStruct((n, d), F32),
        scratch_shapes=[pltpu.VMEM((et, tt), F32), pltpu.VMEM((et, tt), BF16), pltpu.VMEM((d, tt), F32),
                        pltpu.VMEM((PEER_HEADS, tt // LANES, et // PEER_KEYS, LANES), F32),
                        pltpu.VMEM((PEER_HEADS, tt // LANES, et // PEER_KEYS, LANES), F32)],
        compiler_params=pltpu.CompilerParams(dimension_semantics=("parallel", "arbitrary"),
                                             vmem_limit_bytes=VMEM_LIMIT),
        name="experts",
    )(xn2, h, u, vt, r1, n0, e0, e1)


def _proj_columns():
    o_qn, o_qpe, o_ckv, o_kpe = 0, 512, 768, 896
    o_fq, o_fk, o_fv, o_fl = 928, 1440, 1952, 2464
    swap = lambda l: (l + MLA_ROPE // 2) % MLA_ROPE
    cols = list(range(o_qn, o_qn + 512))
    cols += list(range(o_qpe, o_qpe + 256))
    cols += [o_qpe + MLA_ROPE * h + swap(l) for h in range(MLA_HEADS) for l in range(MLA_ROPE)]
    cols += list(range(o_ckv, o_ckv + KV_RANK))
    cols += [o_kpe + l for _ in range(4) for l in range(MLA_ROPE)]
    cols += [o_kpe + swap(l) for _ in range(4) for l in range(MLA_ROPE)]
    cols += list(range(o_fq, o_fq + 512)) + list(range(o_fk, o_fk + 512)) + list(range(o_fv, o_fv + 512))
    cols += list(range(o_fl, o_fl + FOX_HEADS)) + [-1] * (LANES - FOX_HEADS)
    assert len(cols) == C_END
    return np.asarray(cols, np.int32)


def _block_diag_mean(size, group):
    idx = np.arange(size) // group
    return jnp.asarray((idx[:, None] == idx[None, :]).astype(np.float32) / group, BF16)


def _rope_tables(pos):
    half = MLA_ROPE // 2
    inv = jnp.power(ROPE_THETA, -jnp.arange(half, dtype=F32) / half)
    ang = pos.astype(F32)[:, None] * inv
    cos = jnp.concatenate([jnp.cos(ang), jnp.cos(ang)], axis=1)
    sin = jnp.concatenate([-jnp.sin(ang), jnp.sin(ang)], axis=1)
    return (jnp.tile(cos, (1, MLA_HEADS)), jnp.tile(sin, (1, MLA_HEADS)), jnp.tile(cos, (1, 4)), jnp.tile(sin, (1, 4)))


def _tile_for(n, cap):
    t = min(n, cap)
    while n % t:
        t //= 2
    return t


def kernel(x_prompt, x_sample, cache_mla_latent, cache_mla_kpe, cache_fox_k, cache_fox_v, cache_fox_logf, page_table, g_attn, w_in, b_forget, g_q_nope, g_q_pe, g_ckv, g_k_pe, w_ukv, g_k_nope, g_fox_q, g_fox_k, g_out_mla, g_out_fox, w_o, g_ffn, w_peer_q, peer_sub_keys, peer_u, peer_v):
    depth = g_attn.shape[0]
    assert depth == 1, "single-layer trunk"
    b, t, d = x_prompt.shape
    db, dt, _ = x_sample.shape
    assert dt == 1
    n_pages = page_table.shape[1]
    page = cache_mla_latent.shape[2]
    past = n_pages * page
    l = 0

    cols = _proj_columns()
    w_pad = jnp.concatenate([w_in[l], jnp.zeros((d, 1), F32)], axis=1)
    w_arr = w_pad[:, jnp.where(cols < 0, w_in.shape[2], cols)].astype(BF16)
    ukv = w_ukv[l].reshape(KV_RANK, MLA_HEADS, MLA_NOPE + MLA_V)
    wk = ukv[:, :, :MLA_NOPE].reshape(KV_RANK, MLA_HEADS * MLA_NOPE).astype(BF16)
    wv = ukv[:, :, MLA_NOPE:].reshape(KV_RANK, MLA_HEADS * MLA_V).astype(BF16)
    wukv = jnp.concatenate([wk, wv], axis=1)
    swap = (np.arange(MLA_ROPE) + MLA_ROPE // 2) % MLA_ROPE
    s_mla = (MLA_NOPE + MLA_ROPE) ** -0.5
    s_fox = FOX_DIM ** -0.5
    row = lambda v: v.reshape(1, -1).astype(F32)
    consts = [
        row(g_attn[l]), w_arr, wukv,
        row(jnp.tile(g_q_nope[l], MLA_HEADS) * s_mla), row(jnp.tile(g_q_pe[l], MLA_HEADS) * s_mla),
        row(jnp.tile(g_q_pe[l][swap], MLA_HEADS) * s_mla), row(g_ckv[l]),
        row(jnp.tile(g_k_pe[l], 4)), row(jnp.tile(g_k_pe[l][swap], 4)),
        row(jnp.tile(g_fox_q[l], FOX_HEADS) * s_fox), row(jnp.tile(g_fox_k[l], FOX_HEADS)),
        row(jnp.tile(g_k_nope[l], MLA_HEADS)),
        row(jnp.concatenate([b_forget[l], jnp.zeros((LANES - FOX_HEADS,), F32)])),
        _block_diag_mean(256, 64), _block_diag_mean(256, 32), _block_diag_mean(128, 32),
    ]

    n = b * t
    tm = _tile_for(t, 256)
    tabs_p = _rope_tables(jnp.arange(t, dtype=jnp.int32))
    xp = x_prompt.reshape(n, d)
    (qn, qpe, kn, kpe4, vm, fq, fk, fv, ckv_p, kpe_p, fk_p, fv_p, logf_p) = _proj_call(xp, tm, t // tm, consts, tabs_p)
    blk = _tile_for(t, 256)
    tri = jnp.asarray(np.triu(np.ones((blk, blk), np.float32)), BF16)
    c_t = _cumsum_call(logf_p.reshape(b, t, FOX_HEADS).transpose(0, 2, 1), tri)
    ccol = c_t.reshape(b, 4, 2, t).transpose(1, 0, 3, 2).reshape(4, n, 2)
    crow = c_t.reshape(b * 4, 2, t)
    tq = _tile_for(t, 256)
    om_p, of_p = _attn_call(qn, qpe, kn, kpe4, vm, fq, fk, fv, ccol, crow, b, t, tq)

    tabs_s = _rope_tables(jnp.full((db,), past, jnp.int32))
    xs = x_sample.reshape(db, d)
    (qn_s, qpe_s, _, _, _, fq_s, _, _, ckv_s, kpe_s, fk_s, fv_s, logf_s) = _proj_call(xs, db, 1, consts, tabs_s)
    head_of = np.arange(512) // MLA_NOPE
    blockmask = jnp.asarray(head_of[None, :] == np.arange(MLA_HEADS)[:, None])
    qnblk = jnp.where(blockmask[None], qn_s[:, None, :], jnp.zeros((), BF16))
    qpeblk = qpe_s.reshape(db, MLA_HEADS, MLA_ROPE)
    n_slots = max(s for s in (16, 8, 4, 2, 1) if n_pages % s == 0)
    fdim = FOX_HEADS * FOX_DIM
    om_s, of_s = _decode_call(
        page_table, qnblk, qpeblk, fq_s.astype(F32).reshape(db, fdim, 1),
        ckv_s.reshape(db, 1, KV_RANK), kpe_s.reshape(db, MLA_ROPE, 1),
        fk_s.reshape(db, fdim, 1), fv_s.reshape(db, fdim, 1), logf_s.reshape(db, FOX_HEADS, 1),
        wk.T, wv, consts[11],
        cache_mla_latent[l], cache_mla_kpe[l].transpose(0, 2, 1),
        cache_fox_k[l].transpose(0, 2, 3, 1).reshape(-1, fdim, page),
        cache_fox_v[l].transpose(0, 2, 3, 1).reshape(-1, fdim, page),
        cache_fox_logf[l].transpose(0, 2, 1), n_slots)

    wo1 = w_o[l][:512].astype(BF16)
    wo2 = w_o[l][512:].astype(BF16)
    gm, gf, gffn = row(g_out_mla[l]), row(g_out_fox[l]), row(g_ffn[l])
    wqt = w_peer_q[l].T.astype(BF16)
    keys = peer_sub_keys[l].reshape(2 * PEER_HEADS, PEER_KEYS, PEER_DKEY // 2).astype(BF16)
    u = peer_u[l].astype(BF16)
    vt = peer_v[l].T.astype(BF16)

    def tail(om, of, x, tm_o, tt, et):
        h, xn2 = _outproj_call(om, of, x, gm, gf, wo1, wo2, gffn, tm_o)
        pad = (-h.shape[0]) % tt
        if pad:
            h = jnp.pad(h, ((0, pad), (0, 0)))
            xn2 = jnp.pad(xn2, ((0, pad), (0, 0)))
        gates = _route_call(xn2, wqt, keys, tt)
        y = _experts_call(xn2, h, u, vt, *gates, tt, et)
        return y[:x.shape[0]]

    y_p = tail(om_p, of_p, xp, tm, _tile_for(n, 512), 1024)
    y_s = tail(om_s.reshape(db, 512), of_s.reshape(db, 512), xs, db, LANES, 1024)

    shp = lambda a, *s: a.reshape((depth,) + s)
    return (y_p.reshape(b, t, d), y_s.reshape(db, 1, d),
            shp(ckv_p, b, t, KV_RANK), shp(kpe_p, b, t, MLA_ROPE),
            shp(fk_p, b, t, FOX_HEADS, FOX_DIM), shp(fv_p, b, t, FOX_HEADS, FOX_DIM), shp(logf_p, b, t, FOX_HEADS),
            shp(ckv_s, db, 1, KV_RANK), shp(kpe_s, db, 1, MLA_ROPE),
            shp(fk_s, db, 1, FOX_HEADS, FOX_DIM), shp(fv_s, db, 1, FOX_HEADS, FOX_DIM), shp(logf_s, db, 1, FOX_HEADS))
```
